```python
import math
import jax, jax.numpy as jnp
from jax import lax
import numpy as np

D_MODEL = 1024
BATCH = 16
SEQ = 2048
DEPTH = 1

CHUNK = 64
EPS = 1e-6

MLA_HEADS = 8
MLA_Q_RANK = 384
MLA_KV_RANK = 256
MLA_NOPE = 64
MLA_ROPE = 32
MLA_V = 64
MLA_QK = MLA_NOPE + MLA_ROPE
ROPE_THETA = 10000.0
Q_BLOCK = 128

RWKV_HEADS = 8
RWKV_HEAD = 64
RWKV_DIM = RWKV_HEADS * RWKV_HEAD
W_LORA = 64
A_LORA = 64
G_LORA = 128
LN_X_EPS = 64e-5

MIX_WIDTH = MLA_HEADS * MLA_V + RWKV_DIM
MLA_COLS = MLA_Q_RANK + MLA_KV_RANK + MLA_ROPE
RWKV_COLS = 3 * RWKV_DIM + W_LORA + A_LORA + G_LORA
IN_COLS = MLA_COLS + RWKV_COLS
MLA_SPLITS = (MLA_Q_RANK, MLA_Q_RANK + MLA_KV_RANK)
RWKV_SPLITS = (RWKV_DIM, 2 * RWKV_DIM, 3 * RWKV_DIM, 3 * RWKV_DIM + W_LORA, 3 * RWKV_DIM + W_LORA + A_LORA)

PEER_HEADS = 8
N_KEYS = 128
N_EXPERTS = N_KEYS * N_KEYS
PEER_DQ = 256
PEER_DHALF = PEER_DQ // 2
PEER_TOPK = 16
TOK_BLOCK = 128

kernel_name = "hybrid_mla_rwkv7_peer_block"


def rms_norm(x, g, eps=EPS):
    xf = x.astype(jnp.float32)
    y = xf * lax.rsqrt(jnp.mean(xf * xf, axis=-1, keepdims=True) + eps)
    return (y * g.astype(jnp.float32)).astype(x.dtype)


def apply_rope(t, positions):
    half = MLA_ROPE // 2
    inv = ROPE_THETA ** (-jnp.arange(half, dtype=jnp.float32) / half)
    ang = positions.astype(jnp.float32)[..., None] * inv
    cos = jnp.cos(ang)[:, :, None, :]
    sin = jnp.sin(ang)[:, :, None, :]
    tf = t.astype(jnp.float32)
    t1, t2 = tf[..., :half], tf[..., half:]
    out = jnp.concatenate([t1 * cos - t2 * sin, t1 * sin + t2 * cos], axis=-1)
    return out.astype(t.dtype)


def mla_group(p_mla, positions, g_cq, g_ckv, w_uq, w_uk, w_uv, g_qnorm, g_knorm, g_attn_out):
    B, S, _ = p_mla.shape
    c_q, c_kv, k_rope = jnp.split(p_mla, MLA_SPLITS, axis=-1)
    q = (rms_norm(c_q, g_cq) @ w_uq).reshape(B, S, MLA_HEADS, MLA_QK)
    c_kv = rms_norm(c_kv, g_ckv)
    k_nope = (c_kv @ w_uk).reshape(B, S, MLA_HEADS, MLA_NOPE)
    v = (c_kv @ w_uv).reshape(B, S, MLA_HEADS, MLA_V)
    k = jnp.concatenate([k_nope, jnp.broadcast_to(k_rope[:, :, None, :], (B, S, MLA_HEADS, MLA_ROPE))], axis=-1)
    q = rms_norm(q, g_qnorm)
    k = rms_norm(k, g_knorm)
    q = jnp.concatenate([q[..., :MLA_NOPE], apply_rope(q[..., MLA_NOPE:], positions)], axis=-1)
    k = jnp.concatenate([k[..., :MLA_NOPE], apply_rope(k[..., MLA_NOPE:], positions)], axis=-1)
    scale = MLA_QK ** -0.5
    n_blk = S // Q_BLOCK
    q_blocks = q.reshape(B, n_blk, Q_BLOCK, MLA_HEADS, MLA_QK).transpose(1, 0, 2, 3, 4)
    key_chunk = jnp.arange(S) // CHUNK

    def attend(args):
        qb, bi = args
        q_chunk = (bi * Q_BLOCK + jnp.arange(Q_BLOCK)) // CHUNK
        s = jnp.einsum('bqhd,bkhd->bhqk', qb, k).astype(jnp.float32) * scale
        mask = key_chunk[None, :] <= q_chunk[:, None]
        s = jnp.where(mask[None, None], s, -jnp.inf)
        pr = jax.nn.softmax(s, axis=-1).astype(v.dtype)
        return jnp.einsum('bhqk,bkhd->bqhd', pr, v)

    o = lax.map(attend, (q_blocks, jnp.arange(n_blk)))
    o = o.transpose(1, 0, 2, 3, 4).reshape(B, S, MLA_HEADS, MLA_V)
    o = rms_norm(o, g_attn_out.reshape(MLA_HEADS, MLA_V))
    return o.reshape(B, S, MLA_HEADS * MLA_V)


def rwkv7_step(state, inp):
    r_t, w_t, k_t, v_t, kk_t, a_t = inp
    sk = jnp.einsum('bhij,bhj->bhi', state, kk_t)
    state = (state * w_t[:, :, None, :]
             - sk[..., None] * (kk_t * a_t)[:, :, None, :]
             + v_t[..., None] * k_t[:, :, None, :])
    y = jnp.einsum('bhij,bhj->bhi', state, r_t)
    return state, y


def rwkv7_group(p_rwkv, rwkv_mu, w0, w2, a0, a2, g2, k_k, k_a, r_k, ln_x_w, ln_x_b):
    B, S, _ = p_rwkv.shape
    prev = jnp.pad(p_rwkv, ((0, 0), (1, 0), (0, 0)))[:, :S]
    p = p_rwkv + (prev - p_rwkv) * rwkv_mu
    r, k, v, wl, al, gl = jnp.split(p, RWKV_SPLITS, axis=-1)
    w = -jax.nn.softplus(-(w0 + jnp.tanh(wl) @ w2)) - 0.5
    decay = jnp.exp(-jnp.exp(w.astype(jnp.float32)))
    a = jax.nn.sigmoid(a0 + al @ a2)
    g = jax.nn.sigmoid(gl) @ g2
    heads = lambda t: t.astype(jnp.float32).reshape(B, S, RWKV_HEADS, RWKV_HEAD)
    kk = heads(k * k_k)
    kk = kk / jnp.maximum(jnp.sqrt(jnp.sum(kk * kk, axis=-1, keepdims=True)), 1e-12)
    k = k * (1.0 + (a - 1.0) * k_a)
    r_h, k_h, v_h, a_h, w_h = heads(r), heads(k), heads(v), heads(a), heads(decay)
    xs = tuple(jnp.moveaxis(t, 1, 0) for t in (r_h, w_h, k_h, v_h, kk, a_h))
    state0 = jnp.zeros((B, RWKV_HEADS, RWKV_HEAD, RWKV_HEAD), jnp.float32)
    _, y = lax.scan(rwkv7_step, state0, xs)
    y = jnp.moveaxis(y, 0, 1)
    mu = jnp.mean(y, axis=-1, keepdims=True)
    var = jnp.mean(jnp.square(y - mu), axis=-1, keepdims=True)
    y = ((y - mu) * lax.rsqrt(var + LN_X_EPS) * ln_x_w.astype(jnp.float32).reshape(RWKV_HEADS, RWKV_HEAD)
         + ln_x_b.astype(jnp.float32).reshape(RWKV_HEADS, RWKV_HEAD))
    bonus = jnp.sum(r_h * k_h * r_k.astype(jnp.float32), axis=-1, keepdims=True) * v_h
    y = (y + bonus).reshape(B, S, RWKV_DIM).astype(p_rwkv.dtype)
    return y * g


def peer_ffn(h, w_pq, sub_keys, expert_u, expert_v):
    B, S, D = h.shape
    hb_all = h.reshape(B * S // TOK_BLOCK, TOK_BLOCK, D)

    def block(hb):
        q = (hb @ w_pq).reshape(TOK_BLOCK, PEER_HEADS, 2, PEER_DHALF)
        s = jnp.einsum('thpd,hpnd->thpn', q, sub_keys).astype(jnp.float32)
        s1, i1 = lax.top_k(s[:, :, 0], PEER_TOPK)
        s2, i2 = lax.top_k(s[:, :, 1], PEER_TOPK)
        cand_s = (s1[..., :, None] + s2[..., None, :]).reshape(TOK_BLOCK, PEER_HEADS, PEER_TOPK * PEER_TOPK)
        cand_i = (i1[..., :, None] * N_KEYS + i2[..., None, :]).reshape(TOK_BLOCK, PEER_HEADS, PEER_TOPK * PEER_TOPK)
        top_s, pos = lax.top_k(cand_s, PEER_TOPK)
        idx = jnp.take_along_axis(cand_i, pos, axis=-1)
        gate = jax.nn.softmax(top_s, axis=-1).astype(hb.dtype)
        u = expert_u[idx]
        act = jax.nn.gelu(jnp.einsum('thkd,td->thk', u, hb), approximate=False)
        vv = expert_v[idx]
        return jnp.einsum('thk,thkd->td', gate * act, vv)

    out = lax.map(block, hb_all)
    return out.reshape(B, S, D)


def setup_inputs(seed: int = 0) -> dict:
    key = jax.random.key(seed)
    ks = iter(jax.random.split(key, 40))
    f32 = jnp.float32
    nrm = lambda shape, scale: jax.random.normal(next(ks), shape, f32) * scale
    gain = lambda shape: 1.0 + 0.02 * jax.random.normal(next(ks), shape, f32)
    L = DEPTH
    x = jax.random.normal(next(ks), (BATCH, SEQ, D_MODEL), f32)
    offset = jax.random.randint(next(ks), (BATCH, 1), 0, 64, dtype=jnp.int32) * CHUNK
    positions = offset + jnp.arange(SEQ, dtype=jnp.int32)[None, :]
    return {
        'x': x,
        'positions': positions,
        'g_mix': gain((L, D_MODEL)),
        'w_in': nrm((L, D_MODEL, IN_COLS), D_MODEL ** -0.5),
        'rwkv_mu': jax.random.uniform(next(ks), (L, RWKV_COLS), f32),
        'g_cq': gain((L, MLA_Q_RANK)),
        'g_ckv': gain((L, MLA_KV_RANK)),
        'w_uq': nrm((L, MLA_Q_RANK, MLA_HEADS * MLA_QK), MLA_Q_RANK ** -0.5),
        'w_uk': nrm((L, MLA_KV_RANK, MLA_HEADS * MLA_NOPE), MLA_KV_RANK ** -0.5),
        'w_uv': nrm((L, MLA_KV_RANK, MLA_HEADS * MLA_V), MLA_KV_RANK ** -0.5),
        'g_qnorm': gain((L, MLA_QK)),
        'g_knorm': gain((L, MLA_QK)),
        'g_attn_out': gain((L, MLA_HEADS * MLA_V)),
        'w0': -6.0 + 5.0 * jax.random.uniform(next(ks), (L, RWKV_DIM), f32),
        'w2': nrm((L, W_LORA, RWKV_DIM), 0.1),
        'a0': nrm((L, RWKV_DIM), 0.1),
        'a2': nrm((L, A_LORA, RWKV_DIM), 0.1),
        'g2': nrm((L, G_LORA, RWKV_DIM), G_LORA ** -0.5),
        'k_k': 0.85 + nrm((L, RWKV_DIM), 0.05),
        'k_a': 1.0 + nrm((L, RWKV_DIM), 0.05),
        'r_k': nrm((L, RWKV_HEADS, RWKV_HEAD), 0.1),
        'ln_x_w': gain((L, RWKV_DIM)),
        'ln_x_b': nrm((L, RWKV_DIM), 0.02),
        'w_o': nrm((L, MIX_WIDTH, D_MODEL), MIX_WIDTH ** -0.5),
        'g_ffn': gain((L, D_MODEL)),
        'w_pq': nrm((L, D_MODEL, PEER_HEADS * PEER_DQ), D_MODEL ** -0.5),
        'sub_keys': nrm((L, PEER_HEADS, 2, N_KEYS, PEER_DHALF), PEER_DHALF ** -0.5),
        'expert_u': nrm((L, N_EXPERTS, D_MODEL), D_MODEL ** -0.5),
        'expert_v': nrm((L, N_EXPERTS, D_MODEL), PEER_HEADS ** -0.5),
    }


def reference(x, positions, g_mix, w_in, rwkv_mu, g_cq, g_ckv, w_uq, w_uk, w_uv, g_qnorm, g_knorm,
              g_attn_out, w0, w2, a0, a2, g2, k_k, k_a, r_k, ln_x_w, ln_x_b, w_o, g_ffn, w_pq,
              sub_keys, expert_u, expert_v):
    for l in range(DEPTH):
        h = rms_norm(x, g_mix[l])
        p = h @ w_in[l]
        y_a = mla_group(p[..., :MLA_COLS], positions, g_cq[l], g_ckv[l], w_uq[l], w_uk[l], w_uv[l],
                        g_qnorm[l], g_knorm[l], g_attn_out[l])
        y_b = rwkv7_group(p[..., MLA_COLS:], rwkv_mu[l], w0[l], w2[l], a0[l], a2[l], g2[l], k_k[l], k_a[l],
                          r_k[l], ln_x_w[l], ln_x_b[l])
        x = x + jnp.concatenate([y_a, y_b], axis=-1) @ w_o[l]
        x = x + peer_ffn(rms_norm(x, g_ffn[l]), w_pq[l], sub_keys[l], expert_u[l], expert_v[l])
    return x
```

```python
import functools

import jax
import jax.numpy as jnp
from jax import lax
from jax.experimental import pallas as pl
from jax.experimental.pallas import tpu as pltpu

F32 = jnp.float32
BF16 = jnp.bfloat16
I32 = jnp.int32

D_MODEL = 1024
CHUNK = 64
EPS = 1e-6

MLA_HEADS = 8
MLA_Q_RANK = 384
MLA_KV_RANK = 256
MLA_NOPE = 64
MLA_ROPE = 32
MLA_V = 64
MLA_QK = MLA_NOPE + MLA_ROPE
ROPE_THETA = 10000.0
MLA_COLS = MLA_Q_RANK + MLA_KV_RANK + MLA_ROPE
MLA_PAD = 768
HEAD_PAD = 128

RWKV_HEADS = 8
RWKV_HEAD = 64
RWKV_DIM = RWKV_HEADS * RWKV_HEAD
W_LORA = 64
A_LORA = 64
G_LORA = 128
LN_X_EPS = 64e-5
RWKV_COLS = 3 * RWKV_DIM + W_LORA + A_LORA + G_LORA

PEER_HEADS = 8
N_KEYS = 128
PEER_DQ = 256
PEER_DHALF = PEER_DQ // 2
PEER_TOPK = 16

VMEM_LIMIT = 48 * 1024 * 1024


def _cparams(*sem):
    return pltpu.CompilerParams(dimension_semantics=sem, vmem_limit_bytes=VMEM_LIMIT)


def _full(shape):
    n = len(shape)
    return pl.BlockSpec(shape, lambda *_: (0,) * n)


def _dot(a, b):
    return jnp.dot(a, b, preferred_element_type=F32)


def _dot_nt(a, b):
    return lax.dot_general(a, b, (((1,), (1,)), ((), ())), preferred_element_type=F32)


def _inproj_kernel(x_ref, g_ref, w_ref, om_ref, or_ref):
    x = x_ref[...]
    ms = jnp.mean(x * x, axis=-1, keepdims=True)
    h = (x * lax.rsqrt(ms + EPS) * g_ref[...]).astype(BF16)
    om_ref[...] = _dot(h, w_ref[:, :MLA_PAD])
    or_ref[...] = _dot(h, w_ref[:, MLA_PAD:])


def _stage_inproj(x2, g_mix, w_in_p, tm):
    n = x2.shape[0]
    return pl.pallas_call(
        _inproj_kernel,
        grid=(n // tm,),
        in_specs=[
            pl.BlockSpec((tm, D_MODEL), lambda i: (i, 0)),
            _full((1, D_MODEL)),
            _full((D_MODEL, MLA_PAD + RWKV_COLS)),
        ],
        out_specs=[
            pl.BlockSpec((tm, MLA_PAD), lambda i: (i, 0)),
            pl.BlockSpec((tm, RWKV_COLS), lambda i: (i, 0)),
        ],
        out_shape=[
            jax.ShapeDtypeStruct((n, MLA_PAD), F32),
            jax.ShapeDtypeStruct((n, RWKV_COLS), F32),
        ],
        compiler_params=_cparams("parallel"),
        name="inproj",
    )(x2, g_mix.reshape(1, D_MODEL), w_in_p)


def _mla_prep_kernel(pm_ref, pos_ref, inv_ref, m1_ref, m2_ref, gcq_ref, gckv_ref, wuq_ref, wuk_ref, wuv_ref,
                     gq_ref, gk_ref, bones_ref, q_ref, k_ref, v_ref):
    def rms(t, g):
        ms = jnp.mean(t * t, axis=-1, keepdims=True)
        return (t * lax.rsqrt(ms + EPS) * g).astype(BF16)

    cqn = rms(pm_ref[:, :MLA_Q_RANK], gcq_ref[...])
    ckvn = rms(pm_ref[:, MLA_Q_RANK:MLA_Q_RANK + MLA_KV_RANK], gckv_ref[...])
    q = _dot(cqn, wuq_ref[...])
    k = _dot(ckvn, wuk_ref[...])
    v_ref[...] = _dot(ckvn, wuv_ref[...]).astype(BF16)
    kr = pltpu.roll(pm_ref[:, MLA_Q_RANK + MLA_KV_RANK:], MLA_NOPE, axis=1)
    k = k + jnp.concatenate([kr] * MLA_HEADS, axis=1)

    ang = pos_ref[...] * inv_ref[...]
    cos = jnp.cos(ang)
    sin = jnp.sin(ang)
    c_all = jnp.concatenate([cos] * MLA_HEADS, axis=1)
    s1_all = jnp.concatenate([sin * m1_ref[...]] * MLA_HEADS, axis=1)
    s2_all = jnp.concatenate([sin * m2_ref[...]] * MLA_HEADS, axis=1)
    half = MLA_ROPE // 2
    width = MLA_HEADS * HEAD_PAD

    def norm_rope(t, g):
        ss = _dot((t * t).astype(BF16), bones_ref[...])
        t = t * lax.rsqrt(ss * (1.0 / MLA_QK) + EPS) * g
        return t * c_all + pltpu.roll(t, width - half, axis=1) * s1_all + pltpu.roll(t, half, axis=1) * s2_all

    q_ref[...] = (norm_rope(q, gq_ref[...]) * (MLA_QK ** -0.5)).astype(BF16)
    k_ref[...] = norm_rope(k, gk_ref[...]).astype(BF16)


def _stage_mla_prep(pm, posf, consts, tm):
    n = pm.shape[0]
    width = MLA_HEADS * HEAD_PAD
    vw = MLA_HEADS * MLA_V
    row = lambda w: pl.BlockSpec((tm, w), lambda i: (i, 0))
    return pl.pallas_call(
        _mla_prep_kernel,
        grid=(n // tm,),
        in_specs=[row(MLA_PAD), row(1)] + [_full(c.shape) for c in consts],
        out_specs=[row(width), row(width), row(vw)],
        out_shape=[
            jax.ShapeDtypeStruct((n, width), BF16),
            jax.ShapeDtypeStruct((n, width), BF16),
            jax.ShapeDtypeStruct((n, vw), BF16),
        ],
        compiler_params=_cparams("parallel"),
        name="mla_prep",
    )(pm, posf, *consts)


def _attn_kernel(q_ref, k_ref, v_ref, o_ref, *, tq):
    qi = pl.program_id(2)
    row = lax.broadcasted_iota(I32, (tq, tq), 0)
    col = lax.broadcasted_iota(I32, (tq, tq), 1)
    visible = (col // CHUNK) <= (row // CHUNK)
    lane = lax.broadcasted_iota(I32, (tq, 2 * MLA_V), 1)
    res = []
    for j in range(2):
        q = q_ref[:, HEAD_PAD * j:HEAD_PAD * (j + 1)]

        def tile(kt, carry, masked, q=q, j=j):
            m, l, acc = carry
            ks = pl.multiple_of(kt * tq, tq)
            s = _dot_nt(q, k_ref[pl.ds(ks, tq), HEAD_PAD * j:HEAD_PAD * (j + 1)])
            if masked:
                s = jnp.where(visible, s, -jnp.inf)
            m_new = jnp.maximum(m, jnp.max(s, axis=-1, keepdims=True))
            alpha = jnp.exp(m - m_new)
            p = jnp.exp(s - m_new)
            l = alpha * l + jnp.sum(p, axis=-1, keepdims=True)
            acc = alpha * acc + _dot(p.astype(BF16), v_ref[pl.ds(ks, tq), :])
            return m_new, l, acc

        init = (jnp.full((tq, 1), -jnp.inf, F32), jnp.zeros((tq, 1), F32), jnp.zeros((tq, 2 * MLA_V), F32))
        carry = lax.fori_loop(0, qi, functools.partial(tile, masked=False), init)
        _, l, acc = tile(qi, carry, True)
        res.append(acc / l)
    o_ref[...] = jnp.where(lane < MLA_V, res[0], res[1])


def _stage_attn(q, k, v, batch, seq, tq):
    n = q.shape[0]
    nq = seq // tq
    return pl.pallas_call(
        functools.partial(_attn_kernel, tq=tq),
        grid=(batch, MLA_HEADS // 2, nq),
        in_specs=[
            pl.BlockSpec((tq, 2 * HEAD_PAD), lambda b, h, i: (b * nq + i, h)),
            pl.BlockSpec((seq, 2 * HEAD_PAD), lambda b, h, i: (b, h)),
            pl.BlockSpec((seq, 2 * MLA_V), lambda b, h, i: (b, h)),
        ],
        out_specs=pl.BlockSpec((tq, 2 * MLA_V), lambda b, h, i: (b * nq + i, h)),
        out_shape=jax.ShapeDtypeStruct((n, MLA_HEADS * MLA_V), F32),
        compiler_params=_cparams("parallel", "parallel", "arbitrary"),
        name="attn",
    )(q, k, v)


def _dot_hi(a, b):
    return jnp.dot(a, b, preferred_element_type=F32, precision=lax.Precision.HIGHEST)


def _dot_nt_hi(a, b):
    return lax.dot_general(a, b, (((1,), (1,)), ((), ())), preferred_element_type=F32,
                           precision=lax.Precision.HIGHEST)


def _dot_tn_hi(a, b):
    return lax.dot_general(a, b, (((0,), (0,)), ((), ())), preferred_element_type=F32,
                           precision=lax.Precision.HIGHEST)


def _sigmoid(t):
    return 1.0 / (1.0 + jnp.exp(-t))


def _rwkv_prep_kernel(pr_ref, mu_ref, w0_ref, a0_ref, kk_ref, ka_ref, w2_ref, a2_ref, g2_ref, bones_ref,
                      r_o, k_o, v_o, al_o, be_o, lw_o, g_o, carry_ref, *, tiles_per_seq):
    i = pl.program_id(0)
    p = pr_ref[...]
    tm = p.shape[0]

    @pl.when(i % tiles_per_seq == 0)
    def _():
        carry_ref[...] = jnp.zeros_like(carry_ref)

    rowi = lax.broadcasted_iota(I32, p.shape, 0)
    prev = jnp.where(rowi == 0, carry_ref[0:1, :], pltpu.roll(p, 1, axis=0))
    carry_ref[0:1, :] = p[tm - 1:tm, :]
    xs = p + (prev - p) * mu_ref[...]
    d = RWKV_DIM
    r = xs[:, :d]
    k = xs[:, d:2 * d]
    v = xs[:, 2 * d:3 * d]
    z = xs[:, 3 * d:3 * d + W_LORA + A_LORA]
    gl = xs[:, 3 * d + W_LORA + A_LORA:]
    lane = lax.broadcasted_iota(I32, z.shape, 1)
    zt = jnp.where(lane < W_LORA, jnp.tanh(z), z)
    t = -(w0_ref[...] + _dot_hi(zt, w2_ref[...]))
    w = -(jnp.maximum(t, 0.0) + jnp.log1p(jnp.exp(-jnp.abs(t)))) - 0.5
    a = _sigmoid(a0_ref[...] + _dot_hi(zt, a2_ref[...]))
    kk = k * kk_ref[...]
    ss = _dot_hi(kk * kk, bones_ref[...])
    kk = kk / jnp.maximum(jnp.sqrt(ss), 1e-12)
    r_o[...] = r
    k_o[...] = k * (1.0 + (a - 1.0) * ka_ref[...])
    v_o[...] = v
    al_o[...] = -kk
    be_o[...] = kk * a
    lw_o[...] = -jnp.exp(w)
    g_o[...] = _dot(_sigmoid(gl).astype(BF16), g2_ref[...])


def _stage_rwkv_prep(pr, consts, seq, tm):
    n = pr.shape[0]
    d = RWKV_DIM
    row = lambda w: pl.BlockSpec((tm, w), lambda i: (i, 0))
    return pl.pallas_call(
        functools.partial(_rwkv_prep_kernel, tiles_per_seq=seq // tm),
        grid=(n // tm,),
        in_specs=[row(RWKV_COLS)] + [_full(c.shape) for c in consts],
        out_specs=[row(d)] * 7,
        out_shape=[jax.ShapeDtypeStruct((n, d), F32)] * 7,
        scratch_shapes=[pltpu.VMEM((8, RWKV_COLS), F32)],
        compiler_params=_cparams("arbitrary"),
        name="rwkv_prep",
    )(pr, *consts)


def _rwkv_scan_kernel(r_ref, k_ref, v_ref, al_ref, be_ref, lw_ref, g_ref, rk_ref, lnw_ref, lnb_ref, y_ref, s_ref):
    c = r_ref.shape[0]
    hd = RWKV_HEAD

    @pl.when(pl.program_id(1) == 0)
    def _():
        s_ref[...] = jnp.zeros_like(s_ref)

    row = lax.broadcasted_iota(I32, (c, c), 0)
    col = lax.broadcasted_iota(I32, (c, c), 1)
    incl = col <= row
    strict = col < row
    tri = jnp.where(incl, 1.0, 0.0).astype(F32)
    eye = jnp.where(col == row, 1.0, 0.0).astype(F32)
    for h in range(RWKV_HEADS):
        sl = slice(hd * h, hd * (h + 1))
        r = r_ref[:, sl]
        k = k_ref[:, sl]
        v = v_ref[:, sl]
        al = al_ref[:, sl]
        be = be_ref[:, sl]
        lw = lw_ref[:, sl]
        cum = _dot_hi(tri, lw)
        e_neg = jnp.exp(-cum)
        rt = r * jnp.exp(cum)
        kt = k * e_neg
        bt = be * e_neg
        at = al * jnp.exp(cum - lw)
        m_ab = jnp.where(strict, _dot_nt_hi(at, bt), 0.0)
        m_ak = jnp.where(strict, _dot_nt_hi(at, kt), 0.0)
        m_rb = jnp.where(incl, _dot_nt_hi(rt, bt), 0.0)
        m_rk = jnp.where(incl, _dot_nt_hi(rt, kt), 0.0)
        x = eye + m_ab
        pw = m_ab
        steps = max(1, (c - 1).bit_length() - 1)
        for _ in range(steps):
            pw = _dot_hi(pw, pw)
            x = x + _dot_hi(x, pw)
        w_mat = _dot_hi(x, at)
        u0 = _dot_hi(x, _dot_hi(m_ak, v))
        s = s_ref[h]
        u = _dot_nt_hi(w_mat, s) + u0
        y = _dot_nt_hi(rt, s) + _dot_hi(m_rb, u) + _dot_hi(m_rk, v)
        cum_last = cum[c - 1:c, :]
        e_last = jnp.exp(cum_last - cum)
        s_ref[h] = s * jnp.exp(cum_last) + _dot_tn_hi(u, be * e_last) + _dot_tn_hi(v, k * e_last)
        mu = jnp.mean(y, axis=-1, keepdims=True)
        yc = y - mu
        var = jnp.mean(yc * yc, axis=-1, keepdims=True)
        yn = yc * lax.rsqrt(var + LN_X_EPS) * lnw_ref[:, sl] + lnb_ref[:, sl]
        bonus = jnp.sum(r * k * rk_ref[:, sl], axis=-1, keepdims=True) * v
        y_ref[:, sl] = (yn + bonus) * g_ref[:, sl]


def _stage_rwkv_scan(seqs, consts, batch, seq):
    n = batch * seq
    d = RWKV_DIM
    nc = seq // CHUNK
    blk = pl.BlockSpec((CHUNK, d), lambda b, c: (b * nc + c, 0))
    return pl.pallas_call(
        _rwkv_scan_kernel,
        grid=(batch, nc),
        in_specs=[blk] * 7 + [_full((1, d))] * 3,
        out_specs=blk,
        out_shape=jax.ShapeDtypeStruct((n, d), F32),
        scratch_shapes=[pltpu.VMEM((RWKV_HEADS, RWKV_HEAD, RWKV_HEAD), F32)],
        compiler_params=_cparams("parallel", "arbitrary"),
        name="rwkv_scan",
    )(*seqs, *consts)


def _rwkv_branch(pr, rwkv_mu, w0, w2, a0, a2, g2, k_k, k_a, r_k, ln_x_w, ln_x_b, batch, seq):
    d = RWKV_DIM
    row = lambda t: t.reshape(1, -1)
    w2p = jnp.concatenate([w2, jnp.zeros((A_LORA, d), F32)], axis=0)
    a2p = jnp.concatenate([jnp.zeros((W_LORA, d), F32), a2], axis=0)
    bones = jnp.kron(jnp.eye(RWKV_HEADS, dtype=F32), jnp.ones((RWKV_HEAD, RWKV_HEAD), F32))
    consts = (row(rwkv_mu), row(w0), row(a0), row(k_k), row(k_a), w2p, a2p, g2.astype(BF16), bones)
    seqs = _stage_rwkv_prep(pr, consts, seq, min(512, seq))
    return _stage_rwkv_scan(seqs, (row(r_k), row(ln_x_w), row(ln_x_b)), batch, seq)


def _outproj_kernel(x_ref, oa_ref, yb_ref, gao_ref, bones_ref, wo_ref, gffn_ref, wpq_ref, x1_ref, h2_ref, qp_ref):
    oa = oa_ref[...]
    ss = _dot((oa * oa).astype(BF16), bones_ref[...])
    ya = oa * lax.rsqrt(ss * (1.0 / MLA_V) + EPS) * gao_ref[...]
    mix = jnp.concatenate([ya, yb_ref[...]], axis=1).astype(BF16)
    x1 = x_ref[...] + _dot(mix, wo_ref[...])
    x1_ref[...] = x1
    ms = jnp.mean(x1 * x1, axis=-1, keepdims=True)
    h2 = x1 * lax.rsqrt(ms + EPS) * gffn_ref[...]
    h2_ref[...] = h2
    qp_ref[...] = _dot(h2.astype(BF16), wpq_ref[...]).astype(BF16)


def _stage_outproj(x2, o_attn, y_b, g_attn_out, w_o, g_ffn, w_pq, tm):
    n = x2.shape[0]
    qw = PEER_HEADS * PEER_DQ
    aw = MLA_HEADS * MLA_V
    bones = jnp.kron(jnp.eye(MLA_HEADS, dtype=F32), jnp.ones((MLA_V, MLA_V), F32)).astype(BF16)
    row = lambda w: pl.BlockSpec((tm, w), lambda i: (i, 0))
    return pl.pallas_call(
        _outproj_kernel,
        grid=(n // tm,),
        in_specs=[row(D_MODEL), row(aw), row(RWKV_DIM), _full((1, aw)), _full((aw, aw)),
                  _full((D_MODEL, D_MODEL)), _full((1, D_MODEL)), _full((D_MODEL, qw))],
        out_specs=[row(D_MODEL), row(D_MODEL), row(qw)],
        out_shape=[
            jax.ShapeDtypeStruct((n, D_MODEL), F32),
            jax.ShapeDtypeStruct((n, D_MODEL), F32),
            jax.ShapeDtypeStruct((n, qw), BF16),
        ],
        compiler_params=_cparams("parallel"),
        name="outproj",
    )(x2, o_attn, y_b, g_attn_out.reshape(1, aw), bones, w_o.astype(BF16), g_ffn.reshape(1, D_MODEL),
      w_pq.astype(BF16))


def _topk_rows(s, payload=None):
    rows = s.shape[0]
    riota = lax.broadcasted_iota(I32, s.shape, 0)
    vals, ids = [], []
    for _ in range(PEER_TOPK):
        m = jnp.max(s, axis=0, keepdims=True)
        pos = jnp.min(jnp.where(s == m, riota, rows), axis=0, keepdims=True)
        sel = riota == pos
        vals.append(m)
        ids.append(pos if payload is None else jnp.max(jnp.where(sel, payload, -1), axis=0, keepdims=True))
        s = jnp.where(sel, -jnp.inf, s)
    return jnp.concatenate(vals, axis=0), jnp.concatenate(ids, axis=0)


def _peer_route_kernel(qp_ref, keys_ref, idx_ref, gate_ref, ids_t, gate_t):
    h = pl.program_id(1)
    s1 = _dot_nt(keys_ref[0], qp_ref[:, :PEER_DHALF])
    s2 = _dot_nt(keys_ref[1], qp_ref[:, PEER_DHALF:])
    v1, i1 = _topk_rows(s1)
    v2, i2 = _topk_rows(s2)
    cand_s = jnp.concatenate([v1[i:i + 1] + v2 for i in range(PEER_TOPK)], axis=0)
    cand_i = jnp.concatenate([i1[i:i + 1] * N_KEYS + i2 for i in range(PEER_TOPK)], axis=0)
    top_s, ids = _topk_rows(cand_s, cand_i)
    e = jnp.exp(top_s - top_s[0:1])
    gate = e / jnp.sum(e, axis=0, keepdims=True)
    r0 = pl.multiple_of(h * PEER_TOPK, PEER_TOPK)
    ids_t[pl.ds(r0, PEER_TOPK), :] = ids
    gate_t[pl.ds(r0, PEER_TOPK), :] = gate

    @pl.when(h == PEER_HEADS - 1)
    def _():
        idx_ref[...] = ids_t[...].T
        gate_ref[...] = gate_t[...].T


def _stage_peer_route(qp, sub_keys, tt):
    n = qp.shape[0]
    kw = PEER_HEADS * PEER_TOPK
    keys = sub_keys.reshape(PEER_HEADS * 2, N_KEYS, PEER_DHALF).astype(BF16)
    out = pl.BlockSpec((tt, kw), lambda i, h: (i, 0))
    return pl.pallas_call(
        _peer_route_kernel,
        grid=(n // tt, PEER_HEADS),
        in_specs=[
            pl.BlockSpec((tt, PEER_DQ), lambda i, h: (i, h)),
            pl.BlockSpec((2, N_KEYS, PEER_DHALF), lambda i, h: (h, 0, 0)),
        ],
        out_specs=[out, out],
        out_shape=[jax.ShapeDtypeStruct((n, kw), I32), jax.ShapeDtypeStruct((n, kw), F32)],
        scratch_shapes=[pltpu.VMEM((kw, tt), I32), pltpu.VMEM((kw, tt), F32)],
        compiler_params=_cparams("parallel", "arbitrary"),
        name="peer_route",
    )(qp, keys)


EXPERT_ROWS = 4
PAIRS = PEER_HEADS * PEER_TOPK


def _pack_table(t):
    bits = lax.bitcast_convert_type(t.astype(BF16), jnp.uint16).astype(jnp.uint32)
    half = D_MODEL // 2
    packed = bits[:, :half] | (bits[:, half:] << 16)
    return lax.bitcast_convert_type(packed, I32).reshape(t.shape[0], EXPERT_ROWS, 128)


def _unpack(w):
    lo = pltpu.bitcast(w << 16, F32)
    hi = pltpu.bitcast(w & jnp.int32(-65536), F32)
    return lo, hi


def _gelu(t):
    return 0.5 * t * (1.0 + lax.erf(t * (2.0 ** -0.5)))


def _peer_u_kernel(idx_ref, x_ref, gate_ref, tbl_ref, wgt_ref, prod_ref):
    tb = x_ref.shape[0]
    rows = PAIRS * EXPERT_ROWS
    ones = jnp.ones((8, 128), F32)
    gsum = jnp.where(lax.broadcasted_iota(I32, (rows, PAIRS), 0) // EXPERT_ROWS
                     == lax.broadcasted_iota(I32, (rows, PAIRS), 1), 1.0, 0.0).astype(F32)

    def body(t, carry):
        xt = x_ref[t]
        xlo = xt[:EXPERT_ROWS]
        xhi = xt[EXPERT_ROWS:]
        base = t * PAIRS
        for p in range(PAIRS):
            lo, hi = _unpack(tbl_ref[idx_ref[base + p]])
            prod_ref[EXPERT_ROWS * p:EXPERT_ROWS * (p + 1), :] = lo * xlo + hi * xhi
        lane_sums = _dot_nt_hi(ones, prod_ref[...])
        act = _dot_hi(lane_sums, gsum)[0:1]
        wgt_ref[pl.ds(t, 1), :] = gate_ref[pl.ds(t, 1), :] * _gelu(act)
        return carry

    lax.fori_loop(0, tb, body, 0)


def _peer_v_kernel(idx_ref, wgt_ref, x1_ref, tbl_ref, out_ref):
    tb = x1_ref.shape[0]
    nacc = 4

    def body(t, carry):
        base = t * PAIRS
        acc_lo = [jnp.zeros((EXPERT_ROWS, 128), F32) for _ in range(nacc)]
        acc_hi = [jnp.zeros((EXPERT_ROWS, 128), F32) for _ in range(nacc)]
        for p in range(PAIRS):
            lo, hi = _unpack(tbl_ref[idx_ref[base + p]])
            g = wgt_ref[base + p]
            acc_lo[p % nacc] = acc_lo[p % nacc] + g * lo
            acc_hi[p % nacc] = acc_hi[p % nacc] + g * hi
        lo = (acc_lo[0] + acc_lo[1]) + (acc_lo[2] + acc_lo[3])
        hi = (acc_hi[0] + acc_hi[1]) + (acc_hi[2] + acc_hi[3])
        out_ref[t] = x1_ref[t] + jnp.concatenate([lo, hi], axis=0)
        return carry

    lax.fori_loop(0, tb, body, 0)


def _table_spec(n_exp):
    return pl.BlockSpec((n_exp, EXPERT_ROWS, 128), lambda i: (0, 0, 0), pipeline_mode=pl.Buffered(1))


def _smem_spec(tb):
    return pl.BlockSpec((tb * PAIRS,), lambda i: (i,), memory_space=pltpu.SMEM)


def _stage_peer_u(idx, h2, gate, tbl, tb):
    n = h2.shape[0]
    tok = pl.BlockSpec((tb, 8, 128), lambda i: (i, 0, 0))
    row = pl.BlockSpec((tb, PAIRS), lambda i: (i, 0))
    return pl.pallas_call(
        _peer_u_kernel,
        grid=(n // tb,),
        in_specs=[_smem_spec(tb), tok, row, _table_spec(tbl.shape[0])],
        out_specs=row,
        out_shape=jax.ShapeDtypeStruct((n, PAIRS), F32),
        scratch_shapes=[pltpu.VMEM((PAIRS * EXPERT_ROWS, 128), F32)],
        compiler_params=_cparams("parallel"),
        name="peer_u",
    )(idx.reshape(-1), h2.reshape(n, 8, 128), gate, tbl)


def _stage_peer_v(idx, wgt, x1, tbl, tb):
    n = x1.shape[0]
    tok = pl.BlockSpec((tb, 8, 128), lambda i: (i, 0, 0))
    out = pl.pallas_call(
        _peer_v_kernel,
        grid=(n // tb,),
        in_specs=[_smem_spec(tb), _smem_spec(tb), tok, _table_spec(tbl.shape[0])],
        out_specs=tok,
        out_shape=jax.ShapeDtypeStruct((n, 8, 128), F32),
        compiler_params=_cparams("parallel"),
        name="peer_v",
    )(idx.reshape(-1), wgt.reshape(-1), x1.reshape(n, 8, 128), tbl)
    return out.reshape(n, D_MODEL)


def _peer_block(x2, o_attn, y_b, g_attn_out, w_o, g_ffn, w_pq, sub_keys, expert_u, expert_v):
    n = x2.shape[0]
    x1, h2, qp = _stage_outproj(x2, o_attn, y_b, g_attn_out, w_o, g_ffn, w_pq, min(512, n))
    idx, gate = _stage_peer_route(qp, sub_keys, min(256, n))
    wgt = _stage_peer_u(idx, h2, gate, _pack_table(expert_u), min(128, n))
    return _stage_peer_v(idx, wgt, x1, _pack_table(expert_v), min(128, n))


def _mla_consts(g_cq, g_ckv, w_uq, w_uk, w_uv, g_qnorm, g_knorm):
    half = MLA_ROPE // 2
    inv = ROPE_THETA ** (-jnp.arange(half, dtype=F32) / half)
    z = lambda m: jnp.zeros((m,), F32)
    inv128 = jnp.concatenate([z(MLA_NOPE), inv, inv, z(HEAD_PAD - MLA_QK)]).reshape(1, HEAD_PAD)
    o = jnp.ones((half,), F32)
    m1 = jnp.concatenate([z(MLA_NOPE), -o, z(half), z(HEAD_PAD - MLA_QK)]).reshape(1, HEAD_PAD)
    m2 = jnp.concatenate([z(MLA_NOPE), z(half), o, z(HEAD_PAD - MLA_QK)]).reshape(1, HEAD_PAD)
    wuq = jnp.pad(w_uq.reshape(MLA_Q_RANK, MLA_HEADS, MLA_QK), ((0, 0), (0, 0), (0, HEAD_PAD - MLA_QK)))
    wuq = wuq.reshape(MLA_Q_RANK, MLA_HEADS * HEAD_PAD).astype(BF16)
    wuk = jnp.pad(w_uk.reshape(MLA_KV_RANK, MLA_HEADS, MLA_NOPE), ((0, 0), (0, 0), (0, HEAD_PAD - MLA_NOPE)))
    wuk = wuk.reshape(MLA_KV_RANK, MLA_HEADS * HEAD_PAD).astype(BF16)
    pad_g = lambda g: jnp.tile(jnp.pad(g, (0, HEAD_PAD - MLA_QK)), MLA_HEADS).reshape(1, MLA_HEADS * HEAD_PAD)
    bones = jnp.kron(jnp.eye(MLA_HEADS, dtype=F32), jnp.ones((HEAD_PAD, HEAD_PAD), F32)).astype(BF16)
    return (inv128, m1, m2, g_cq.reshape(1, -1), g_ckv.reshape(1, -1), wuq, wuk, w_uv.astype(BF16),
            pad_g(g_qnorm), pad_g(g_knorm), bones)


def _pad_w_in(w_in):
    z = jnp.zeros((D_MODEL, MLA_PAD - MLA_COLS), w_in.dtype)
    return jnp.concatenate([w_in[:, :MLA_COLS], z, w_in[:, MLA_COLS:]], axis=1).astype(BF16)


def _mla_branch(pm, positions, g_cq, g_ckv, w_uq, w_uk, w_uv, g_qnorm, g_knorm, batch, seq):
    n = batch * seq
    posf = positions.astype(F32).reshape(n, 1)
    consts = _mla_consts(g_cq, g_ckv, w_uq, w_uk, w_uv, g_qnorm, g_knorm)
    q, k, v = _stage_mla_prep(pm, posf, consts, min(512, n))
    return _stage_attn(q, k, v, batch, seq, min(256, seq))


def kernel(x, positions, g_mix, w_in, rwkv_mu, g_cq, g_ckv, w_uq, w_uk, w_uv, g_qnorm, g_knorm, g_attn_out, w0, w2, a0, a2, g2, k_k, k_a, r_k, ln_x_w, ln_x_b, w_o, g_ffn, w_pq, sub_keys, expert_u, expert_v):
    batch, seq, _ = x.shape
    n = batch * seq
    x2 = x.reshape(n, D_MODEL)
    pm, pr = _stage_inproj(x2, g_mix[0], _pad_w_in(w_in[0]), min(512, n))
    o_attn = _mla_branch(pm, positions, g_cq[0], g_ckv[0], w_uq[0], w_uk[0], w_uv[0], g_qnorm[0], g_knorm[0],
                         batch, seq)
    y_b = _rwkv_branch(pr, rwkv_mu[0], w0[0], w2[0], a0[0], a2[0], g2[0], k_k[0], k_a[0], r_k[0], ln_x_w[0],
                       ln_x_b[0], batch, seq)
    out = _peer_block(x2, o_attn, y_b, g_attn_out[0], w_o[0], g_ffn[0], w_pq[0], sub_keys[0], expert_u[0],
                      expert_v[0])
    return out.reshape(batch, seq, D_MODEL)
```

```python
import functools

import jax
import jax.numpy as jnp
from jax import lax
from jax.experimental import pallas as pl
from jax.experimental.pallas import tpu as pltpu

F32 = jnp.float32
BF16 = jnp.bfloat16
I32 = jnp.int32

D_MODEL = 1024
CHUNK = 64
EPS = 1e-6

MLA_HEADS = 8
MLA_Q_RANK = 384
MLA_KV_RANK = 256
MLA_NOPE = 64
MLA_ROPE = 32
MLA_V = 64
MLA_QK = MLA_NOPE + MLA_ROPE
ROPE_THETA = 10000.0
MLA_COLS = MLA_Q_RANK + MLA_KV_RANK + MLA_ROPE
MLA_PAD = 768
HEAD_PAD = 128

RWKV_HEADS = 8
RWKV_HEAD = 64
RWKV_DIM = RWKV_HEADS * RWKV_HEAD
W_LORA = 64
A_LORA = 64
G_LORA = 128
LN_X_EPS = 64e-5
RWKV_COLS = 3 * RWKV_DIM + W_LORA + A_LORA + G_LORA

PEER_HEADS = 8
N_KEYS = 128
PEER_DQ = 256
PEER_DHALF = PEER_DQ // 2
PEER_TOPK = 16

VMEM_LIMIT = 48 * 1024 * 1024


def _cparams(*sem):
    return pltpu.CompilerParams(dimension_semantics=sem, vmem_limit_bytes=VMEM_LIMIT)


def _full(shape):
    n = len(shape)
    return pl.BlockSpec(shape, lambda *_: (0,) * n)


def _dot(a, b):
    return jnp.dot(a, b, preferred_element_type=F32)


def _dot_nt(a, b):
    return lax.dot_general(a, b, (((1,), (1,)), ((), ())), preferred_element_type=F32)


def _inproj_kernel(x_ref, g_ref, w_ref, om_ref, or_ref):
    x = x_ref[...]
    ms = jnp.mean(x * x, axis=-1, keepdims=True)
    h = (x * lax.rsqrt(ms + EPS) * g_ref[...]).astype(BF16)
    om_ref[...] = _dot(h, w_ref[:, :MLA_PAD])
    or_ref[...] = _dot(h, w_ref[:, MLA_PAD:])


def _stage_inproj(x2, g_mix, w_in_p, tm):
    n = x2.shape[0]
    return pl.pallas_call(
        _inproj_kernel,
        grid=(n // tm,),
        in_specs=[
            pl.BlockSpec((tm, D_MODEL), lambda i: (i, 0)),
            _full((1, D_MODEL)),
            _full((D_MODEL, MLA_PAD + RWKV_COLS)),
        ],
        out_specs=[
            pl.BlockSpec((tm, MLA_PAD), lambda i: (i, 0)),
            pl.BlockSpec((tm, RWKV_COLS), lambda i: (i, 0)),
        ],
        out_shape=[
            jax.ShapeDtypeStruct((n, MLA_PAD), F32),
            jax.ShapeDtypeStruct((n, RWKV_COLS), F32),
        ],
        compiler_params=_cparams("parallel"),
        name="inproj",
    )(x2, g_mix.reshape(1, D_MODEL), w_in_p)


def _mla_prep_kernel(pm_ref, pos_ref, inv_ref, m1_ref, m2_ref, gcq_ref, gckv_ref, wuq_ref, wuk_ref, wuv_ref,
                     gq_ref, gk_ref, bones_ref, q_ref, k_ref, v_ref):
    def rms(t, g):
        ms = jnp.mean(t * t, axis=-1, keepdims=True)
        return (t * lax.rsqrt(ms + EPS) * g).astype(BF16)

    cqn = rms(pm_ref[:, :MLA_Q_RANK], gcq_ref[...])
    ckvn = rms(pm_ref[:, MLA_Q_RANK:MLA_Q_RANK + MLA_KV_RANK], gckv_ref[...])
    q = _dot(cqn, wuq_ref[...])
    k = _dot(ckvn, wuk_ref[...])
    v_ref[...] = _dot(ckvn, wuv_ref[...]).astype(BF16)
    kr = pltpu.roll(pm_ref[:, MLA_Q_RANK + MLA_KV_RANK:], MLA_NOPE, axis=1)
    k = k + jnp.concatenate([kr] * MLA_HEADS, axis=1)

    ang = pos_ref[...] * inv_ref[...]
    cos = jnp.cos(ang)
    sin = jnp.sin(ang)
    c_all = jnp.concatenate([cos] * MLA_HEADS, axis=1)
    s1_all = jnp.concatenate([sin * m1_ref[...]] * MLA_HEADS, axis=1)
    s2_all = jnp.concatenate([sin * m2_ref[...]] * MLA_HEADS, axis=1)
    half = MLA_ROPE // 2
    width = MLA_HEADS * HEAD_PAD

    def norm_rope(t, g):
        ss = _dot((t * t).astype(BF16), bones_ref[...])
        t = t * lax.rsqrt(ss * (1.0 / MLA_QK) + EPS) * g
        return t * c_all + pltpu.roll(t, width - half, axis=1) * s1_all + pltpu.roll(t, half, axis=1) * s2_all

    q_ref[...] = (norm_rope(q, gq_ref[...]) * (MLA_QK ** -0.5)).astype(BF16)
    k_ref[...] = norm_rope(k, gk_ref[...]).astype(BF16)


def _stage_mla_prep(pm, posf, consts, tm):
    n = pm.shape[0]
    width = MLA_HEADS * HEAD_PAD
    vw = MLA_HEADS * MLA_V
    row = lambda w: pl.BlockSpec((tm, w), lambda i: (i, 0))
    return pl.pallas_call(
        _mla_prep_kernel,
        grid=(n // tm,),
        in_specs=[row(MLA_PAD), row(1)] + [_full(c.shape) for c in consts],
        out_specs=[row(width), row(width), row(vw)],
        out_shape=[
            jax.ShapeDtypeStruct((n, width), BF16),
            jax.ShapeDtypeStruct((n, width), BF16),
            jax.ShapeDtypeStruct((n, vw), BF16),
        ],
        compiler_params=_cparams("parallel"),
        name="mla_prep",
    )(pm, posf, *consts)


def _attn_kernel(q_ref, k_ref, v_ref, o_ref, *, tq):
    qi = pl.program_id(2)
    row = lax.broadcasted_iota(I32, (tq, tq), 0)
    col = lax.broadcasted_iota(I32, (tq, tq), 1)
    visible = (col // CHUNK) <= (row // CHUNK)
    lane = lax.broadcasted_iota(I32, (tq, 2 * MLA_V), 1)
    res = []
    for j in range(2):
        q = q_ref[:, HEAD_PAD * j:HEAD_PAD * (j + 1)]

        def tile(kt, carry, masked, q=q, j=j):
            m, l, acc = carry
            ks = pl.multiple_of(kt * tq, tq)
            s = _dot_nt(q, k_ref[pl.ds(ks, tq), HEAD_PAD * j:HEAD_PAD * (j + 1)])
            if masked:
                s = jnp.where(visible, s, -jnp.inf)
            m_new = jnp.maximum(m, jnp.max(s, axis=-1, keepdims=True))
            alpha = jnp.exp(m - m_new)
            p = jnp.exp(s - m_new)
            l = alpha * l + jnp.sum(p, axis=-1, keepdims=True)
            acc = alpha * acc + _dot(p.astype(BF16), v_ref[pl.ds(ks, tq), :])
            return m_new, l, acc

        init = (jnp.full((tq, 1), -jnp.inf, F32), jnp.zeros((tq, 1), F32), jnp.zeros((tq, 2 * MLA_V), F32))
        carry = lax.fori_loop(0, qi, functools.partial(tile, masked=False), init)
        _, l, acc = tile(qi, carry, True)
        res.append(acc / l)
    o_ref[...] = jnp.where(lane < MLA_V, res[0], res[1])


def _stage_attn(q, k, v, batch, seq, tq):
    n = q.shape[0]
    nq = seq // tq
    return pl.pallas_call(
        functools.partial(_attn_kernel, tq=tq),
        grid=(batch, MLA_HEADS // 2, nq),
        in_specs=[
            pl.BlockSpec((tq, 2 * HEAD_PAD), lambda b, h, i: (b * nq + i, h)),
            pl.BlockSpec((seq, 2 * HEAD_PAD), lambda b, h, i: (b, h)),
            pl.BlockSpec((seq, 2 * MLA_V), lambda b, h, i: (b, h)),
        ],
        out_specs=pl.BlockSpec((tq, 2 * MLA_V), lambda b, h, i: (b * nq + i, h)),
        out_shape=jax.ShapeDtypeStruct((n, MLA_HEADS * MLA_V), F32),
        compiler_params=_cparams("parallel", "parallel", "arbitrary"),
        name="attn",
    )(q, k, v)


def _dot_hi(a, b):
    return jnp.dot(a, b, preferred_element_type=F32, precision=lax.Precision.HIGHEST)


def _sigmoid(t):
    return 1.0 / (1.0 + jnp.exp(-t))


def _rwkv_prep_kernel(pr_ref, mu_ref, w0_ref, a0_ref, kk_ref, ka_ref, w2_ref, a2_ref, g2_ref, bones_ref,
                      r_o, k_o, v_o, al_o, be_o, lw_o, g_o, carry_ref, *, tiles_per_seq):
    i = pl.program_id(0)
    p = pr_ref[...]
    tm = p.shape[0]

    @pl.when(i % tiles_per_seq == 0)
    def _():
        carry_ref[...] = jnp.zeros_like(carry_ref)

    rowi = lax.broadcasted_iota(I32, p.shape, 0)
    prev = jnp.where(rowi == 0, carry_ref[0:1, :], pltpu.roll(p, 1, axis=0))
    carry_ref[0:1, :] = p[tm - 1:tm, :]
    xs = p + (prev - p) * mu_ref[...]
    d = RWKV_DIM
    r = xs[:, :d]
    k = xs[:, d:2 * d]
    v = xs[:, 2 * d:3 * d]
    z = xs[:, 3 * d:3 * d + W_LORA + A_LORA]
    gl = xs[:, 3 * d + W_LORA + A_LORA:]
    lane = lax.broadcasted_iota(I32, z.shape, 1)
    zt = jnp.where(lane < W_LORA, jnp.tanh(z), z)
    t = -(w0_ref[...] + _dot_hi(zt, w2_ref[...]))
    w = -(jnp.maximum(t, 0.0) + jnp.log1p(jnp.exp(-jnp.abs(t)))) - 0.5
    a = _sigmoid(a0_ref[...] + _dot_hi(zt, a2_ref[...]))
    kk = k * kk_ref[...]
    ss = _dot_hi(kk * kk, bones_ref[...])
    kk = kk / jnp.maximum(jnp.sqrt(ss), 1e-12)
    r_o[...] = r
    k_o[...] = k * (1.0 + (a - 1.0) * ka_ref[...])
    v_o[...] = v
    al_o[...] = -kk
    be_o[...] = kk * a
    lw_o[...] = -jnp.exp(w)
    g_o[...] = _dot(_sigmoid(gl).astype(BF16), g2_ref[...])


def _stage_rwkv_prep(pr, consts, seq, tm):
    n = pr.shape[0]
    d = RWKV_DIM
    row = lambda w: pl.BlockSpec((tm, w), lambda i: (i, 0))
    return pl.pallas_call(
        functools.partial(_rwkv_prep_kernel, tiles_per_seq=seq // tm),
        grid=(n // tm,),
        in_specs=[row(RWKV_COLS)] + [_full(c.shape) for c in consts],
        out_specs=[row(d)] * 7,
        out_shape=[jax.ShapeDtypeStruct((n, d), F32)] * 7,
        scratch_shapes=[pltpu.VMEM((8, RWKV_COLS), F32)],
        compiler_params=_cparams("arbitrary"),
        name="rwkv_prep",
    )(pr, *consts)


def _mm(a, b):
    return jnp.dot(a.astype(BF16), b.astype(BF16), preferred_element_type=F32)


def _mm_nt(a, b):
    return _dot_nt(a.astype(BF16), b.astype(BF16))


def _mm_tn(a, b):
    return lax.dot_general(a.astype(BF16), b.astype(BF16), (((0,), (0,)), ((), ())), preferred_element_type=F32)


SCAN_CHUNKS = 2


def _rwkv_scan_kernel(r_ref, k_ref, v_ref, al_ref, be_ref, lw_ref, g_ref, rk_ref, lnw_ref, lnb_ref, y_ref, s_ref):
    c = CHUNK
    hd = RWKV_HEAD

    @pl.when(pl.program_id(1) == 0)
    def _():
        s_ref[...] = jnp.zeros_like(s_ref)

    row = lax.broadcasted_iota(I32, (c, c), 0)
    col = lax.broadcasted_iota(I32, (c, c), 1)
    tri = jnp.where(col <= row, 1.0, 0.0).astype(F32)
    eye = jnp.where(col == row, 1.0, 0.0).astype(F32)
    row2 = lax.broadcasted_iota(I32, (2 * c, 2 * c), 0)
    col2 = lax.broadcasted_iota(I32, (2 * c, 2 * c), 1) % c
    keep = col2 < jnp.where(row2 < c, row2, row2 - c + 1)
    heads = range(RWKV_HEADS)
    units = [(h, g) for g in range(SCAN_CHUNKS) for h in heads]
    cat = lambda a, b: jnp.concatenate([a, b], axis=0)

    def lanes(h):
        return slice(hd * h, hd * (h + 1))

    def load(ref, u):
        return ref[c * u[1]:c * (u[1] + 1), lanes(u[0])]

    lw = {u: load(lw_ref, u) for u in units}
    cum = {u: _dot_hi(tri, lw[u]) for u in units}
    k = {u: load(k_ref, u) for u in units}
    v = {u: load(v_ref, u) for u in units}
    be = {u: load(be_ref, u) for u in units}
    rt = {u: load(r_ref, u) * jnp.exp(cum[u]) for u in units}
    at = {u: load(al_ref, u) * jnp.exp(cum[u] - lw[u]) for u in units}
    e_neg = {u: jnp.exp(-cum[u]) for u in units}
    m = {u: jnp.where(keep, _mm_nt(cat(at[u], rt[u]), cat(be[u] * e_neg[u], k[u] * e_neg[u])), 0.0) for u in units}
    m_top = {u: m[u][:c] for u in units}
    m_bot = {u: m[u][c:] for u in units}
    pw = {u: m_top[u][:, :c] for u in units}
    x = {u: eye + pw[u] for u in units}
    for _ in range(max(1, (c - 1).bit_length() - 1)):
        pw = {u: _mm(pw[u], pw[u]) for u in units}
        x = {u: x[u] + _mm(x[u], pw[u]) for u in units}
    w_mat = {u: _mm(x[u], at[u]) for u in units}
    akv = {u: _mm(m_top[u], cat(jnp.zeros_like(v[u]), v[u])) for u in units}
    u0 = {u: _mm(x[u], akv[u]) for u in units}
    cum_last = {u: cum[u][c - 1:c, :] for u in units}
    bk = {}
    for u in units:
        e_last = jnp.exp(cum_last[u] - cum[u])
        bk[u] = cat(be[u] * e_last, k[u] * e_last)

    s = [s_ref[h] for h in heads]
    for g in range(SCAN_CHUNKS):
        ws = [_mm_nt(cat(w_mat[h, g], rt[h, g]), s[h]) for h in heads]
        uv = [cat(ws[h][:c] + u0[h, g], v[h, g]) for h in heads]
        y = [ws[h][c:] + _mm(m_bot[h, g], uv[h]) for h in heads]
        s = [s[h] * jnp.exp(cum_last[h, g]) + _mm_tn(uv[h], bk[h, g]) for h in heads]
        for h in heads:
            u = (h, g)
            sl = lanes(h)
            mu = jnp.mean(y[h], axis=-1, keepdims=True)
            yc = y[h] - mu
            var = jnp.mean(yc * yc, axis=-1, keepdims=True)
            yn = yc * lax.rsqrt(var + LN_X_EPS) * lnw_ref[:, sl] + lnb_ref[:, sl]
            bonus = jnp.sum(load(r_ref, u) * k[u] * rk_ref[:, sl], axis=-1, keepdims=True) * v[u]
            y_ref[c * g:c * (g + 1), sl] = (yn + bonus) * load(g_ref, u)
    for h in heads:
        s_ref[h] = s[h]


def _stage_rwkv_scan(seqs, consts, batch, seq):
    n = batch * seq
    d = RWKV_DIM
    step = CHUNK * SCAN_CHUNKS
    nc = seq // step
    blk = pl.BlockSpec((step, d), lambda b, c: (b * nc + c, 0))
    return pl.pallas_call(
        _rwkv_scan_kernel,
        grid=(batch, nc),
        in_specs=[blk] * 7 + [_full((1, d))] * 3,
        out_specs=blk,
        out_shape=jax.ShapeDtypeStruct((n, d), F32),
        scratch_shapes=[pltpu.VMEM((RWKV_HEADS, RWKV_HEAD, RWKV_HEAD), F32)],
        compiler_params=_cparams("parallel", "arbitrary"),
        name="rwkv_scan",
    )(*seqs, *consts)


def _rwkv_branch(pr, rwkv_mu, w0, w2, a0, a2, g2, k_k, k_a, r_k, ln_x_w, ln_x_b, batch, seq):
    d = RWKV_DIM
    row = lambda t: t.reshape(1, -1)
    w2p = jnp.concatenate([w2, jnp.zeros((A_LORA, d), F32)], axis=0)
    a2p = jnp.concatenate([jnp.zeros((W_LORA, d), F32), a2], axis=0)
    bones = jnp.kron(jnp.eye(RWKV_HEADS, dtype=F32), jnp.ones((RWKV_HEAD, RWKV_HEAD), F32))
    consts = (row(rwkv_mu), row(w0), row(a0), row(k_k), row(k_a), w2p, a2p, g2.astype(BF16), bones)
    seqs = _stage_rwkv_prep(pr, consts, seq, min(512, seq))
    return _stage_rwkv_scan(seqs, (row(r_k), row(ln_x_w), row(ln_x_b)), batch, seq)


def _outproj_kernel(x_ref, oa_ref, yb_ref, gao_ref, bones_ref, wo_ref, gffn_ref, wpq_ref, x1_ref, h2_ref, qp_ref):
    oa = oa_ref[...]
    ss = _dot((oa * oa).astype(BF16), bones_ref[...])
    ya = oa * lax.rsqrt(ss * (1.0 / MLA_V) + EPS) * gao_ref[...]
    mix = jnp.concatenate([ya, yb_ref[...]], axis=1).astype(BF16)
    x1 = x_ref[...] + _dot(mix, wo_ref[...])
    x1_ref[...] = x1
    ms = jnp.mean(x1 * x1, axis=-1, keepdims=True)
    h2 = x1 * lax.rsqrt(ms + EPS) * gffn_ref[...]
    h2_ref[...] = h2
    qp_ref[...] = _dot(h2.astype(BF16), wpq_ref[...]).astype(BF16)


def _stage_outproj(x2, o_attn, y_b, g_attn_out, w_o, g_ffn, w_pq, tm):
    n = x2.shape[0]
    qw = PEER_HEADS * PEER_DQ
    aw = MLA_HEADS * MLA_V
    bones = jnp.kron(jnp.eye(MLA_HEADS, dtype=F32), jnp.ones((MLA_V, MLA_V), F32)).astype(BF16)
    row = lambda w: pl.BlockSpec((tm, w), lambda i: (i, 0))
    return pl.pallas_call(
        _outproj_kernel,
        grid=(n // tm,),
        in_specs=[row(D_MODEL), row(aw), row(RWKV_DIM), _full((1, aw)), _full((aw, aw)),
                  _full((D_MODEL, D_MODEL)), _full((1, D_MODEL)), _full((D_MODEL, qw))],
        out_specs=[row(D_MODEL), row(D_MODEL), row(qw)],
        out_shape=[
            jax.ShapeDtypeStruct((n, D_MODEL), F32),
            jax.ShapeDtypeStruct((n, D_MODEL), F32),
            jax.ShapeDtypeStruct((n, qw), BF16),
        ],
        compiler_params=_cparams("parallel"),
        name="outproj",
    )(x2, o_attn, y_b, g_attn_out.reshape(1, aw), bones, w_o.astype(BF16), g_ffn.reshape(1, D_MODEL),
      w_pq.astype(BF16))


def _topk_rows(s, payload=None):
    rows = s.shape[0]
    riota = lax.broadcasted_iota(I32, s.shape, 0)
    vals, ids = [], []
    for _ in range(PEER_TOPK):
        m = jnp.max(s, axis=0, keepdims=True)
        pos = jnp.min(jnp.where(s == m, riota, rows), axis=0, keepdims=True)
        sel = riota == pos
        vals.append(m)
        ids.append(pos if payload is None else jnp.max(jnp.where(sel, payload, -1), axis=0, keepdims=True))
        s = jnp.where(sel, -jnp.inf, s)
    return jnp.concatenate(vals, axis=0), jnp.concatenate(ids, axis=0)


def _peer_route_kernel(qp_ref, keys_ref, idx_ref, gate_ref, ids_t, gate_t):
    h = pl.program_id(1)
    s1 = _dot_nt(keys_ref[0], qp_ref[:, :PEER_DHALF])
    s2 = _dot_nt(keys_ref[1], qp_ref[:, PEER_DHALF:])
    v1, i1 = _topk_rows(s1)
    v2, i2 = _topk_rows(s2)
    cand_s = jnp.concatenate([v1[i:i + 1] + v2 for i in range(PEER_TOPK)], axis=0)
    cand_i = jnp.concatenate([i1[i:i + 1] * N_KEYS + i2 for i in range(PEER_TOPK)], axis=0)
    top_s, ids = _topk_rows(cand_s, cand_i)
    e = jnp.exp(top_s - top_s[0:1])
    gate = e / jnp.sum(e, axis=0, keepdims=True)
    r0 = pl.multiple_of(h * PEER_TOPK, PEER_TOPK)
    ids_t[pl.ds(r0, PEER_TOPK), :] = ids
    gate_t[pl.ds(r0, PEER_TOPK), :] = gate

    @pl.when(h == PEER_HEADS - 1)
    def _():
        idx_ref[...] = ids_t[...].T
        gate_ref[...] = gate_t[...].T


def _stage_peer_route(qp, sub_keys, tt):
    n = qp.shape[0]
    kw = PEER_HEADS * PEER_TOPK
    keys = sub_keys.reshape(PEER_HEADS * 2, N_KEYS, PEER_DHALF).astype(BF16)
    out = pl.BlockSpec((tt, kw), lambda i, h: (i, 0))
    return pl.pallas_call(
        _peer_route_kernel,
        grid=(n // tt, PEER_HEADS),
        in_specs=[
            pl.BlockSpec((tt, PEER_DQ), lambda i, h: (i, h)),
            pl.BlockSpec((2, N_KEYS, PEER_DHALF), lambda i, h: (h, 0, 0)),
        ],
        out_specs=[out, out],
        out_shape=[jax.ShapeDtypeStruct((n, kw), I32), jax.ShapeDtypeStruct((n, kw), F32)],
        scratch_shapes=[pltpu.VMEM((kw, tt), I32), pltpu.VMEM((kw, tt), F32)],
        compiler_params=_cparams("parallel", "arbitrary"),
        name="peer_route",
    )(qp, keys)


EXPERT_ROWS = 4
PAIRS = PEER_HEADS * PEER_TOPK


def _pack_table(t):
    bits = lax.bitcast_convert_type(t.astype(BF16), jnp.uint16).astype(jnp.uint32)
    half = D_MODEL // 2
    packed = bits[:, :half] | (bits[:, half:] << 16)
    return lax.bitcast_convert_type(packed, I32).reshape(t.shape[0], EXPERT_ROWS, 128)


def _unpack(w):
    lo = pltpu.bitcast(w << 16, F32)
    hi = pltpu.bitcast(w & jnp.int32(-65536), F32)
    return lo, hi


def _gelu(t):
    return 0.5 * t * (1.0 + lax.erf(t * (2.0 ** -0.5)))


PAIR_CHUNK = 16


def _peer_u_kernel(idx_ref, x_ref, gate_ref, tbl_ref, wgt_ref, prod_ref, psum_ref, act_ref):
    tb = x_ref.shape[0]
    chunk_rows = PAIR_CHUNK * EXPERT_ROWS

    def token(t, carry):
        xt = x_ref[t]
        xlo = xt[:EXPERT_ROWS]
        xhi = xt[EXPERT_ROWS:]
        base = t * PAIRS

        def chunk(c, carry2):
            off = base + c * PAIR_CHUNK
            row0 = pl.multiple_of(c * chunk_rows, chunk_rows)
            for j in range(PAIR_CHUNK):
                lo, hi = _unpack(tbl_ref[idx_ref[off + j]])
                prod_ref[pl.ds(row0 + EXPERT_ROWS * j, EXPERT_ROWS), :] = lo * xlo + hi * xhi
            return carry2

        lax.fori_loop(0, PAIRS // PAIR_CHUNK, chunk, 0)
        part = prod_ref[pl.ds(0, PAIRS, stride=EXPERT_ROWS), :]
        for s in range(1, EXPERT_ROWS):
            part = part + prod_ref[pl.ds(s, PAIRS, stride=EXPERT_ROWS), :]
        psum_ref[pl.ds(pl.multiple_of(t * PAIRS, PAIRS), PAIRS), :] = part
        return carry

    lax.fori_loop(0, tb, token, 0)
    ps = psum_ref[...]
    hi = ps.astype(BF16)
    lo = (ps - hi.astype(F32)).astype(BF16)
    ones = jnp.ones((8, 128), BF16)
    sums = _dot_nt(ones, hi) + _dot_nt(ones, lo)
    for t in range(tb):
        act_ref[t:t + 1, :] = sums[0:1, PAIRS * t:PAIRS * (t + 1)]
    wgt_ref[...] = gate_ref[...] * _gelu(act_ref[...])


def _peer_v_kernel(idx_ref, wgt_ref, x1_ref, tbl_ref, out_ref):
    tb = x1_ref.shape[0]

    def token(t, carry):
        base = t * PAIRS

        def chunk(c, accs):
            off = base + c * PAIR_CHUNK
            accs = list(accs)
            for j in range(PAIR_CHUNK):
                lo, hi = _unpack(tbl_ref[idx_ref[off + j]])
                g = wgt_ref[off + j]
                k = 2 * (j % 2)
                accs[k] = accs[k] + g * lo
                accs[k + 1] = accs[k + 1] + g * hi
            return tuple(accs)

        zero = jnp.zeros((EXPERT_ROWS, 128), F32)
        accs = lax.fori_loop(0, PAIRS // PAIR_CHUNK, chunk, (zero, zero, zero, zero))
        out_ref[t] = x1_ref[t] + jnp.concatenate([accs[0] + accs[2], accs[1] + accs[3]], axis=0)
        return carry

    lax.fori_loop(0, tb, token, 0)


def _table_spec(n_exp):
    return pl.BlockSpec((n_exp, EXPERT_ROWS, 128), lambda i: (0, 0, 0), pipeline_mode=pl.Buffered(1))


def _smem_spec(tb):
    return pl.BlockSpec((tb * PAIRS,), lambda i: (i,), memory_space=pltpu.SMEM)


def _stage_peer_u(idx, h2, gate, tbl, tb):
    n = h2.shape[0]
    tok = pl.BlockSpec((tb, 8, 128), lambda i: (i, 0, 0))
    row = pl.BlockSpec((tb, PAIRS), lambda i: (i, 0))
    return pl.pallas_call(
        _peer_u_kernel,
        grid=(n // tb,),
        in_specs=[_smem_spec(tb), tok, row, _table_spec(tbl.shape[0])],
        out_specs=row,
        out_shape=jax.ShapeDtypeStruct((n, PAIRS), F32),
        scratch_shapes=[pltpu.VMEM((PAIRS * EXPERT_ROWS, 128), F32), pltpu.VMEM((tb * PAIRS, 128), F32),
                        pltpu.VMEM((tb, PAIRS), F32)],
        compiler_params=_cparams("parallel"),
        name="peer_u",
    )(idx.reshape(-1), h2.reshape(n, 8, 128), gate, tbl)


def _stage_peer_v(idx, wgt, x1, tbl, tb):
    n = x1.shape[0]
    tok = pl.BlockSpec((tb, 8, 128), lambda i: (i, 0, 0))
    out = pl.pallas_call(
        _peer_v_kernel,
        grid=(n // tb,),
        in_specs=[_smem_spec(tb), _smem_spec(tb), tok, _table_spec(tbl.shape[0])],
        out_specs=tok,
        out_shape=jax.ShapeDtypeStruct((n, 8, 128), F32),
        compiler_params=_cparams("parallel"),
        name="peer_v",
    )(idx.reshape(-1), wgt.reshape(-1), x1.reshape(n, 8, 128), tbl)
    return out.reshape(n, D_MODEL)


def _peer_block(x2, o_attn, y_b, g_attn_out, w_o, g_ffn, w_pq, sub_keys, expert_u, expert_v):
    n = x2.shape[0]
    x1, h2, qp = _stage_outproj(x2, o_attn, y_b, g_attn_out, w_o, g_ffn, w_pq, min(512, n))
    idx, gate = _stage_peer_route(qp, sub_keys, min(256, n))
    wgt = _stage_peer_u(idx, h2, gate, _pack_table(expert_u), min(128, n))
    return _stage_peer_v(idx, wgt, x1, _pack_table(expert_v), min(128, n))


def _mla_consts(g_cq, g_ckv, w_uq, w_uk, w_uv, g_qnorm, g_knorm):
    half = MLA_ROPE // 2
    inv = ROPE_THETA ** (-jnp.arange(half, dtype=F32) / half)
    z = lambda m: jnp.zeros((m,), F32)
    inv128 = jnp.concatenate([z(MLA_NOPE), inv, inv, z(HEAD_PAD - MLA_QK)]).reshape(1, HEAD_PAD)
    o = jnp.ones((half,), F32)
    m1 = jnp.concatenate([z(MLA_NOPE), -o, z(half), z(HEAD_PAD - MLA_QK)]).reshape(1, HEAD_PAD)
    m2 = jnp.concatenate([z(MLA_NOPE), z(half), o, z(HEAD_PAD - MLA_QK)]).reshape(1, HEAD_PAD)
    wuq = jnp.pad(w_uq.reshape(MLA_Q_RANK, MLA_HEADS, MLA_QK), ((0, 0), (0, 0), (0, HEAD_PAD - MLA_QK)))
    wuq = wuq.reshape(MLA_Q_RANK, MLA_HEADS * HEAD_PAD).astype(BF16)
    wuk = jnp.pad(w_uk.reshape(MLA_KV_RANK, MLA_HEADS, MLA_NOPE), ((0, 0), (0, 0), (0, HEAD_PAD - MLA_NOPE)))
    wuk = wuk.reshape(MLA_KV_RANK, MLA_HEADS * HEAD_PAD).astype(BF16)
    pad_g = lambda g: jnp.tile(jnp.pad(g, (0, HEAD_PAD - MLA_QK)), MLA_HEADS).reshape(1, MLA_HEADS * HEAD_PAD)
    bones = jnp.kron(jnp.eye(MLA_HEADS, dtype=F32), jnp.ones((HEAD_PAD, HEAD_PAD), F32)).astype(BF16)
    return (inv128, m1, m2, g_cq.reshape(1, -1), g_ckv.reshape(1, -1), wuq, wuk, w_uv.astype(BF16),
            pad_g(g_qnorm), pad_g(g_knorm), bones)


def _pad_w_in(w_in):
    z = jnp.zeros((D_MODEL, MLA_PAD - MLA_COLS), w_in.dtype)
    return jnp.concatenate([w_in[:, :MLA_COLS], z, w_in[:, MLA_COLS:]], axis=1).astype(BF16)


def _mla_branch(pm, positions, g_cq, g_ckv, w_uq, w_uk, w_uv, g_qnorm, g_knorm, batch, seq):
    n = batch * seq
    posf = positions.astype(F32).reshape(n, 1)
    consts = _mla_consts(g_cq, g_ckv, w_uq, w_uk, w_uv, g_qnorm, g_knorm)
    q, k, v = _stage_mla_prep(pm, posf, consts, min(512, n))
    return _stage_attn(q, k, v, batch, seq, min(256, seq))


def kernel(x, positions, g_mix, w_in, rwkv_mu, g_cq, g_ckv, w_uq, w_uk, w_uv, g_qnorm, g_knorm, g_attn_out, w0, w2, a0, a2, g2, k_k, k_a, r_k, ln_x_w, ln_x_b, w_o, g_ffn, w_pq, sub_keys, expert_u, expert_v):
    batch, seq, _ = x.shape
    n = batch * seq
    x2 = x.reshape(n, D_MODEL)
    pm, pr = _stage_inproj(x2, g_mix[0], _pad_w_in(w_in[0]), min(512, n))
    o_attn = _mla_branch(pm, positions, g_cq[0], g_ckv[0], w_uq[0], w_uk[0], w_uv[0], g_qnorm[0], g_knorm[0],
                         batch, seq)
    y_b = _rwkv_branch(pr, rwkv_mu[0], w0[0], w2[0], a0[0], a2[0], g2[0], k_k[0], k_a[0], r_k[0], ln_x_w[0],
                       ln_x_b[0], batch, seq)
    out = _peer_block(x2, o_attn, y_b, g_attn_out[0], w_o[0], g_ffn[0], w_pq[0], sub_keys[0], expert_u[0],
                      expert_v[0])
    return out.reshape(batch, seq, D_MODEL)
```

```python
import functools

import jax
import jax.numpy as jnp
from jax import lax
from jax.experimental import pallas as pl
from jax.experimental.pallas import tpu as pltpu

F32 = jnp.float32
BF16 = jnp.bfloat16
I32 = jnp.int32

D_MODEL = 1024
CHUNK = 64
EPS = 1e-6

MLA_HEADS = 8
MLA_Q_RANK = 384
MLA_KV_RANK = 256
MLA_NOPE = 64
MLA_ROPE = 32
MLA_V = 64
MLA_QK = MLA_NOPE + MLA_ROPE
ROPE_THETA = 10000.0
MLA_COLS = MLA_Q_RANK + MLA_KV_RANK + MLA_ROPE
MLA_PAD = 768
HEAD_PAD = 128

RWKV_HEADS = 8
RWKV_HEAD = 64
RWKV_DIM = RWKV_HEADS * RWKV_HEAD
W_LORA = 64
A_LORA = 64
G_LORA = 128
LN_X_EPS = 64e-5
RWKV_COLS = 3 * RWKV_DIM + W_LORA + A_LORA + G_LORA

PEER_HEADS = 8
N_KEYS = 128
PEER_DQ = 256
PEER_DHALF = PEER_DQ // 2
PEER_TOPK = 16

VMEM_LIMIT = 48 * 1024 * 1024


def _cparams(*sem):
    return pltpu.CompilerParams(dimension_semantics=sem, vmem_limit_bytes=VMEM_LIMIT)


def _full(shape):
    n = len(shape)
    return pl.BlockSpec(shape, lambda *_: (0,) * n)


def _dot(a, b):
    return jnp.dot(a, b, preferred_element_type=F32)


def _dot_nt(a, b):
    return lax.dot_general(a, b, (((1,), (1,)), ((), ())), preferred_element_type=F32)


def _inproj_kernel(x_ref, g_ref, w_ref, om_ref, or_ref):
    x = x_ref[...]
    ms = jnp.mean(x * x, axis=-1, keepdims=True)
    h = (x * lax.rsqrt(ms + EPS) * g_ref[...]).astype(BF16)
    om_ref[...] = _dot(h, w_ref[:, :MLA_PAD])
    or_ref[...] = _dot(h, w_ref[:, MLA_PAD:])


def _stage_inproj(x2, g_mix, w_in_p, tm):
    n = x2.shape[0]
    return pl.pallas_call(
        _inproj_kernel,
        grid=(n // tm,),
        in_specs=[
            pl.BlockSpec((tm, D_MODEL), lambda i: (i, 0)),
            _full((1, D_MODEL)),
            _full((D_MODEL, MLA_PAD + RWKV_COLS)),
        ],
        out_specs=[
            pl.BlockSpec((tm, MLA_PAD), lambda i: (i, 0)),
            pl.BlockSpec((tm, RWKV_COLS), lambda i: (i, 0)),
        ],
        out_shape=[
            jax.ShapeDtypeStruct((n, MLA_PAD), F32),
            jax.ShapeDtypeStruct((n, RWKV_COLS), F32),
        ],
        compiler_params=_cparams("parallel"),
        name="inproj",
    )(x2, g_mix.reshape(1, D_MODEL), w_in_p)


def _mla_prep_kernel(pm_ref, pos_ref, inv_ref, m1_ref, m2_ref, gcq_ref, gckv_ref, wuq_ref, wuk_ref, wuv_ref,
                     gq_ref, gk_ref, bones_ref, q_ref, k_ref, v_ref):
    def rms(t, g):
        ms = jnp.mean(t * t, axis=-1, keepdims=True)
        return (t * lax.rsqrt(ms + EPS) * g).astype(BF16)

    cqn = rms(pm_ref[:, :MLA_Q_RANK], gcq_ref[...])
    ckvn = rms(pm_ref[:, MLA_Q_RANK:MLA_Q_RANK + MLA_KV_RANK], gckv_ref[...])
    q = _dot(cqn, wuq_ref[...])
    k = _dot(ckvn, wuk_ref[...])
    v_ref[...] = _dot(ckvn, wuv_ref[...]).astype(BF16)
    kr = pltpu.roll(pm_ref[:, MLA_Q_RANK + MLA_KV_RANK:], MLA_NOPE, axis=1)
    k = k + jnp.concatenate([kr] * MLA_HEADS, axis=1)

    ang = pos_ref[...] * inv_ref[...]
    cos = jnp.cos(ang)
    sin = jnp.sin(ang)
    c_all = jnp.concatenate([cos] * MLA_HEADS, axis=1)
    s1_all = jnp.concatenate([sin * m1_ref[...]] * MLA_HEADS, axis=1)
    s2_all = jnp.concatenate([sin * m2_ref[...]] * MLA_HEADS, axis=1)
    half = MLA_ROPE // 2
    width = MLA_HEADS * HEAD_PAD

    def norm_rope(t, g):
        ss = _dot((t * t).astype(BF16), bones_ref[...])
        t = t * lax.rsqrt(ss * (1.0 / MLA_QK) + EPS) * g
        return t * c_all + pltpu.roll(t, width - half, axis=1) * s1_all + pltpu.roll(t, half, axis=1) * s2_all

    q_ref[...] = (norm_rope(q, gq_ref[...]) * (MLA_QK ** -0.5)).astype(BF16)
    k_ref[...] = norm_rope(k, gk_ref[...]).astype(BF16)


def _stage_mla_prep(pm, posf, consts, tm):
    n = pm.shape[0]
    width = MLA_HEADS * HEAD_PAD
    vw = MLA_HEADS * MLA_V
    row = lambda w: pl.BlockSpec((tm, w), lambda i: (i, 0))
    return pl.pallas_call(
        _mla_prep_kernel,
        grid=(n // tm,),
        in_specs=[row(MLA_PAD), row(1)] + [_full(c.shape) for c in consts],
        out_specs=[row(width), row(width), row(vw)],
        out_shape=[
            jax.ShapeDtypeStruct((n, width), BF16),
            jax.ShapeDtypeStruct((n, width), BF16),
            jax.ShapeDtypeStruct((n, vw), BF16),
        ],
        compiler_params=_cparams("parallel"),
        name="mla_prep",
    )(pm, posf, *consts)


def _attn_kernel(q_ref, k_ref, v_ref, o_ref, *, tq, tk):
    qi = pl.program_id(2)
    sub = tq // tk
    qs = [q_ref[:, HEAD_PAD * j:HEAD_PAD * (j + 1)] for j in range(2)]

    def tile(kt, carry, mask):
        ks = pl.multiple_of(kt * tk, tk)
        vb = v_ref[pl.ds(ks, tk), :]
        out = []
        for j in range(2):
            m, l, acc = carry[j]
            s = _dot_nt(qs[j], k_ref[pl.ds(ks, tk), HEAD_PAD * j:HEAD_PAD * (j + 1)])
            if mask is not None:
                s = jnp.where(mask, s, -jnp.inf)
            m_new = jnp.maximum(m, jnp.max(s, axis=-1, keepdims=True))
            alpha = jnp.exp(m - m_new)
            p = jnp.exp(s - m_new)
            l = alpha * l + jnp.sum(p, axis=-1, keepdims=True)
            acc = alpha * acc + _dot(p.astype(BF16), vb)
            out.append((m_new, l, acc))
        return tuple(out)

    one = (jnp.full((tq, 1), -jnp.inf, F32), jnp.zeros((tq, 1), F32), jnp.zeros((tq, 2 * MLA_V), F32))
    carry = lax.fori_loop(0, qi * sub, functools.partial(tile, mask=None), (one, one))
    row = lax.broadcasted_iota(I32, (tq, tk), 0)
    col = lax.broadcasted_iota(I32, (tq, tk), 1)
    for d in range(sub):
        carry = tile(qi * sub + d, carry, ((col + d * tk) // CHUNK) <= (row // CHUNK))
    lane = lax.broadcasted_iota(I32, (tq, 2 * MLA_V), 1)
    o_ref[...] = jnp.where(lane < MLA_V, carry[0][2] / carry[0][1], carry[1][2] / carry[1][1])


def _stage_attn(q, k, v, batch, seq, tq, tk):
    n = q.shape[0]
    nq = seq // tq
    return pl.pallas_call(
        functools.partial(_attn_kernel, tq=tq, tk=tk),
        grid=(batch, MLA_HEADS // 2, nq),
        in_specs=[
            pl.BlockSpec((tq, 2 * HEAD_PAD), lambda b, h, i: (b * nq + i, h)),
            pl.BlockSpec((seq, 2 * HEAD_PAD), lambda b, h, i: (b, h)),
            pl.BlockSpec((seq, 2 * MLA_V), lambda b, h, i: (b, h)),
        ],
        out_specs=pl.BlockSpec((tq, 2 * MLA_V), lambda b, h, i: (b * nq + i, h)),
        out_shape=jax.ShapeDtypeStruct((n, MLA_HEADS * MLA_V), F32),
        compiler_params=_cparams("parallel", "parallel", "arbitrary"),
        name="attn",
    )(q, k, v)


def _dot_hi(a, b):
    return jnp.dot(a, b, preferred_element_type=F32, precision=lax.Precision.HIGHEST)


def _sigmoid(t):
    return 1.0 / (1.0 + jnp.exp(-t))


def _rwkv_prep_kernel(pr_ref, mu_ref, w0_ref, a0_ref, kk_ref, ka_ref, w2_ref, a2_ref, g2_ref, bones_ref,
                      r_o, k_o, v_o, al_o, be_o, lw_o, g_o, carry_ref, *, tiles_per_seq):
    i = pl.program_id(0)
    p = pr_ref[...]
    tm = p.shape[0]

    @pl.when(i % tiles_per_seq == 0)
    def _():
        carry_ref[...] = jnp.zeros_like(carry_ref)

    rowi = lax.broadcasted_iota(I32, p.shape, 0)
    prev = jnp.where(rowi == 0, carry_ref[0:1, :], pltpu.roll(p, 1, axis=0))
    carry_ref[0:1, :] = p[tm - 1:tm, :]
    xs = p + (prev - p) * mu_ref[...]
    d = RWKV_DIM
    r = xs[:, :d]
    k = xs[:, d:2 * d]
    v = xs[:, 2 * d:3 * d]
    z = xs[:, 3 * d:3 * d + W_LORA + A_LORA]
    gl = xs[:, 3 * d + W_LORA + A_LORA:]
    lane = lax.broadcasted_iota(I32, z.shape, 1)
    zt = jnp.where(lane < W_LORA, jnp.tanh(z), z)
    t = -(w0_ref[...] + _dot_hi(zt, w2_ref[...]))
    w = -(jnp.maximum(t, 0.0) + jnp.log1p(jnp.exp(-jnp.abs(t)))) - 0.5
    a = _sigmoid(a0_ref[...] + _dot_hi(zt, a2_ref[...]))
    kk = k * kk_ref[...]
    ss = _dot_hi(kk * kk, bones_ref[...])
    kk = kk / jnp.maximum(jnp.sqrt(ss), 1e-12)
    r_o[...] = r
    k_o[...] = k * (1.0 + (a - 1.0) * ka_ref[...])
    v_o[...] = v
    al_o[...] = -kk
    be_o[...] = kk * a
    lw_o[...] = -jnp.exp(w)
    g_o[...] = _dot(_sigmoid(gl).astype(BF16), g2_ref[...])


def _stage_rwkv_prep(pr, consts, seq, tm):
    n = pr.shape[0]
    d = RWKV_DIM
    row = lambda w: pl.BlockSpec((tm, w), lambda i: (i, 0))
    return pl.pallas_call(
        functools.partial(_rwkv_prep_kernel, tiles_per_seq=seq // tm),
        grid=(n // tm,),
        in_specs=[row(RWKV_COLS)] + [_full(c.shape) for c in consts],
        out_specs=[row(d)] * 7,
        out_shape=[jax.ShapeDtypeStruct((n, d), F32)] * 7,
        scratch_shapes=[pltpu.VMEM((8, RWKV_COLS), F32)],
        compiler_params=_cparams("arbitrary"),
        name="rwkv_prep",
    )(pr, *consts)


def _mm(a, b):
    return jnp.dot(a.astype(BF16), b.astype(BF16), preferred_element_type=F32)


def _mm_nt(a, b):
    return _dot_nt(a.astype(BF16), b.astype(BF16))


def _mm_tn(a, b):
    return lax.dot_general(a.astype(BF16), b.astype(BF16), (((0,), (0,)), ((), ())), preferred_element_type=F32)


SCAN_CHUNKS = 4


def _rwkv_scan_kernel(r_ref, k_ref, v_ref, al_ref, be_ref, lw_ref, g_ref, rk_ref, lnw_ref, lnb_ref, y_ref, s_ref):
    c = CHUNK
    hd = RWKV_HEAD

    @pl.when(pl.program_id(1) == 0)
    def _():
        s_ref[...] = jnp.zeros_like(s_ref)

    row = lax.broadcasted_iota(I32, (c, c), 0)
    col = lax.broadcasted_iota(I32, (c, c), 1)
    tri = jnp.where(col <= row, 1.0, 0.0).astype(F32)
    eye = jnp.where(col == row, 1.0, 0.0).astype(F32)
    row2 = lax.broadcasted_iota(I32, (2 * c, 2 * c), 0)
    col2 = lax.broadcasted_iota(I32, (2 * c, 2 * c), 1) % c
    keep = col2 < jnp.where(row2 < c, row2, row2 - c + 1)
    heads = range(RWKV_HEADS)
    units = [(h, g) for g in range(SCAN_CHUNKS) for h in heads]
    cat = lambda a, b: jnp.concatenate([a, b], axis=0)

    def lanes(h):
        return slice(hd * h, hd * (h + 1))

    def load(ref, u):
        return ref[c * u[1]:c * (u[1] + 1), lanes(u[0])]

    lw = {u: load(lw_ref, u) for u in units}
    cum = {u: _dot_hi(tri, lw[u]) for u in units}
    k = {u: load(k_ref, u) for u in units}
    v = {u: load(v_ref, u) for u in units}
    be = {u: load(be_ref, u) for u in units}
    rt = {u: load(r_ref, u) * jnp.exp(cum[u]) for u in units}
    at = {u: load(al_ref, u) * jnp.exp(cum[u] - lw[u]) for u in units}
    e_neg = {u: jnp.exp(-cum[u]) for u in units}
    m = {u: jnp.where(keep, _mm_nt(cat(at[u], rt[u]), cat(be[u] * e_neg[u], k[u] * e_neg[u])), 0.0) for u in units}
    m_top = {u: m[u][:c] for u in units}
    m_bot = {u: m[u][c:] for u in units}
    pw = {u: m_top[u][:, :c] for u in units}
    x = {u: eye + pw[u] for u in units}
    for _ in range(max(1, (c - 1).bit_length() - 1)):
        pw = {u: _mm(pw[u], pw[u]) for u in units}
        x = {u: x[u] + _mm(x[u], pw[u]) for u in units}
    w_mat = {u: _mm(x[u], at[u]) for u in units}
    akv = {u: _mm(m_top[u], cat(jnp.zeros_like(v[u]), v[u])) for u in units}
    u0 = {u: _mm(x[u], akv[u]) for u in units}
    cum_last = {u: cum[u][c - 1:c, :] for u in units}
    bk = {}
    for u in units:
        e_last = jnp.exp(cum_last[u] - cum[u])
        bk[u] = cat(be[u] * e_last, k[u] * e_last)

    s = [s_ref[h] for h in heads]
    for g in range(SCAN_CHUNKS):
        ws = [_mm_nt(cat(w_mat[h, g], rt[h, g]), s[h]) for h in heads]
        uv = [cat(ws[h][:c] + u0[h, g], v[h, g]) for h in heads]
        y = [ws[h][c:] + _mm(m_bot[h, g], uv[h]) for h in heads]
        s = [s[h] * jnp.exp(cum_last[h, g]) + _mm_tn(uv[h], bk[h, g]) for h in heads]
        for h in heads:
            u = (h, g)
            sl = lanes(h)
            mu = jnp.mean(y[h], axis=-1, keepdims=True)
            yc = y[h] - mu
            var = jnp.mean(yc * yc, axis=-1, keepdims=True)
            yn = yc * lax.rsqrt(var + LN_X_EPS) * lnw_ref[:, sl] + lnb_ref[:, sl]
            bonus = jnp.sum(load(r_ref, u) * k[u] * rk_ref[:, sl], axis=-1, keepdims=True) * v[u]
            y_ref[c * g:c * (g + 1), sl] = (yn + bonus) * load(g_ref, u)
    for h in heads:
        s_ref[h] = s[h]


def _stage_rwkv_scan(seqs, consts, batch, seq):
    n = batch * seq
    d = RWKV_DIM
    step = CHUNK * SCAN_CHUNKS
    nc = seq // step
    blk = pl.BlockSpec((step, d), lambda b, c: (b * nc + c, 0))
    return pl.pallas_call(
        _rwkv_scan_kernel,
        grid=(batch, nc),
        in_specs=[blk] * 7 + [_full((1, d))] * 3,
        out_specs=blk,
        out_shape=jax.ShapeDtypeStruct((n, d), F32),
        scratch_shapes=[pltpu.VMEM((RWKV_HEADS, RWKV_HEAD, RWKV_HEAD), F32)],
        compiler_params=_cparams("parallel", "arbitrary"),
        name="rwkv_scan",
    )(*seqs, *consts)


def _rwkv_branch(pr, rwkv_mu, w0, w2, a0, a2, g2, k_k, k_a, r_k, ln_x_w, ln_x_b, batch, seq):
    d = RWKV_DIM
    row = lambda t: t.reshape(1, -1)
    w2p = jnp.concatenate([w2, jnp.zeros((A_LORA, d), F32)], axis=0)
    a2p = jnp.concatenate([jnp.zeros((W_LORA, d), F32), a2], axis=0)
    bones = jnp.kron(jnp.eye(RWKV_HEADS, dtype=F32), jnp.ones((RWKV_HEAD, RWKV_HEAD), F32))
    consts = (row(rwkv_mu), row(w0), row(a0), row(k_k), row(k_a), w2p, a2p, g2.astype(BF16), bones)
    seqs = _stage_rwkv_prep(pr, consts, seq, min(512, seq))
    return _stage_rwkv_scan(seqs, (row(r_k), row(ln_x_w), row(ln_x_b)), batch, seq)


def _outproj_kernel(x_ref, oa_ref, yb_ref, gao_ref, bones_ref, wo_ref, gffn_ref, wpq_ref, x1_ref, h2_ref, qp_ref):
    oa = oa_ref[...]
    ss = _dot((oa * oa).astype(BF16), bones_ref[...])
    ya = oa * lax.rsqrt(ss * (1.0 / MLA_V) + EPS) * gao_ref[...]
    mix = jnp.concatenate([ya, yb_ref[...]], axis=1).astype(BF16)
    x1 = x_ref[...] + _dot(mix, wo_ref[...])
    x1_ref[...] = x1
    ms = jnp.mean(x1 * x1, axis=-1, keepdims=True)
    h2 = x1 * lax.rsqrt(ms + EPS) * gffn_ref[...]
    h2_ref[...] = h2
    qp_ref[...] = _dot(h2.astype(BF16), wpq_ref[...]).astype(BF16)


def _stage_outproj(x2, o_attn, y_b, g_attn_out, w_o, g_ffn, w_pq, tm):
    n = x2.shape[0]
    qw = PEER_HEADS * PEER_DQ
    aw = MLA_HEADS * MLA_V
    bones = jnp.kron(jnp.eye(MLA_HEADS, dtype=F32), jnp.ones((MLA_V, MLA_V), F32)).astype(BF16)
    row = lambda w: pl.BlockSpec((tm, w), lambda i: (i, 0))
    return pl.pallas_call(
        _outproj_kernel,
        grid=(n // tm,),
        in_specs=[row(D_MODEL), row(aw), row(RWKV_DIM), _full((1, aw)), _full((aw, aw)),
                  _full((D_MODEL, D_MODEL)), _full((1, D_MODEL)), _full((D_MODEL, qw))],
        out_specs=[row(D_MODEL), row(D_MODEL), row(qw)],
        out_shape=[
            jax.ShapeDtypeStruct((n, D_MODEL), F32),
            jax.ShapeDtypeStruct((n, D_MODEL), F32),
            jax.ShapeDtypeStruct((n, qw), BF16),
        ],
        compiler_params=_cparams("parallel"),
        name="outproj",
    )(x2, o_attn, y_b, g_attn_out.reshape(1, aw), bones, w_o.astype(BF16), g_ffn.reshape(1, D_MODEL),
      w_pq.astype(BF16))


def _topk_rows(s, payload=None):
    rows = s.shape[0]
    riota = lax.broadcasted_iota(I32, s.shape, 0)
    vals, ids = [], []
    for _ in range(PEER_TOPK):
        m = jnp.max(s, axis=0, keepdims=True)
        pos = jnp.min(jnp.where(s == m, riota, rows), axis=0, keepdims=True)
        sel = riota == pos
        vals.append(m)
        ids.append(pos if payload is None else jnp.max(jnp.where(sel, payload, -1), axis=0, keepdims=True))
        s = jnp.where(sel, -jnp.inf, s)
    return jnp.concatenate(vals, axis=0), jnp.concatenate(ids, axis=0)


def _peer_route_kernel(qp_ref, keys_ref, idx_ref, gate_ref, ids_t, gate_t):
    h = pl.program_id(1)
    s1 = _dot_nt(keys_ref[0], qp_ref[:, :PEER_DHALF])
    s2 = _dot_nt(keys_ref[1], qp_ref[:, PEER_DHALF:])
    v1, i1 = _topk_rows(s1)
    v2, i2 = _topk_rows(s2)
    sub8 = lax.broadcasted_iota(I32, (8, s1.shape[1]), 0)
    i1 = i1 * (N_KEYS * EXPERT_ROWS)
    i2 = i2 * EXPERT_ROWS
    blocks_s = [v1[0:1] + v2]
    blocks_i = [i1[0:1] + i2]
    for i in range(1, 8):
        valid = sub8 < PEER_TOPK // (i + 1)
        blocks_s.append(jnp.where(valid, v1[i:i + 1] + v2[:8], -jnp.inf))
        blocks_i.append(i1[i:i + 1] + i2[:8])
    blocks_s.append(v1[8:] + v2[0:1])
    blocks_i.append(i1[8:] + i2[0:1])
    top_s, ids = _topk_rows(jnp.concatenate(blocks_s, axis=0), jnp.concatenate(blocks_i, axis=0))
    e = jnp.exp(top_s - top_s[0:1])
    gate = e / jnp.sum(e, axis=0, keepdims=True)
    r0 = pl.multiple_of(h * PEER_TOPK, PEER_TOPK)
    ids_t[pl.ds(r0, PEER_TOPK), :] = ids
    gate_t[pl.ds(r0, PEER_TOPK), :] = gate

    @pl.when(h == PEER_HEADS - 1)
    def _():
        idx_ref[...] = ids_t[...].T
        gate_ref[...] = gate_t[...].T


def _stage_peer_route(qp, sub_keys, tt):
    n = qp.shape[0]
    kw = PEER_HEADS * PEER_TOPK
    keys = sub_keys.reshape(PEER_HEADS * 2, N_KEYS, PEER_DHALF).astype(BF16)
    out = pl.BlockSpec((tt, kw), lambda i, h: (i, 0))
    return pl.pallas_call(
        _peer_route_kernel,
        grid=(n // tt, PEER_HEADS),
        in_specs=[
            pl.BlockSpec((tt, PEER_DQ), lambda i, h: (i, h)),
            pl.BlockSpec((2, N_KEYS, PEER_DHALF), lambda i, h: (h, 0, 0)),
        ],
        out_specs=[out, out],
        out_shape=[jax.ShapeDtypeStruct((n, kw), I32), jax.ShapeDtypeStruct((n, kw), F32)],
        scratch_shapes=[pltpu.VMEM((kw, tt), I32), pltpu.VMEM((kw, tt), F32)],
        compiler_params=_cparams("parallel", "arbitrary"),
        name="peer_route",
    )(qp, keys)


EXPERT_ROWS = 4
PAIRS = PEER_HEADS * PEER_TOPK


def _pack_table(t):
    bits = lax.bitcast_convert_type(t.astype(BF16), jnp.uint16).astype(jnp.uint32)
    half = D_MODEL // 2
    packed = bits[:, :half] | (bits[:, half:] << 16)
    return lax.bitcast_convert_type(packed, I32).reshape(t.shape[0] * EXPERT_ROWS, 128)


def _expert(tbl_ref, row):
    return tbl_ref[pl.ds(pl.multiple_of(row, EXPERT_ROWS), EXPERT_ROWS), :]


def _unpack(w):
    lo = pltpu.bitcast(w << 16, F32)
    hi = pltpu.bitcast(w & jnp.int32(-65536), F32)
    return lo, hi


def _gelu(t):
    return 0.5 * t * (1.0 + lax.erf(t * (2.0 ** -0.5)))


PAIR_CHUNK = 16


def _peer_u_kernel(idx_ref, x_ref, gate_ref, tbl_ref, wgt_ref, prod_ref, psum_ref, act_ref):
    tb = x_ref.shape[0]
    chunk_rows = PAIR_CHUNK * EXPERT_ROWS

    def token(t, carry):
        xt = x_ref[t]
        xlo = xt[:EXPERT_ROWS]
        xhi = xt[EXPERT_ROWS:]
        base = t * PAIRS

        def chunk(c, carry2):
            off = base + c * PAIR_CHUNK
            row0 = pl.multiple_of(c * chunk_rows, chunk_rows)
            for j in range(PAIR_CHUNK):
                lo, hi = _unpack(_expert(tbl_ref, idx_ref[off + j]))
                prod_ref[pl.ds(row0 + EXPERT_ROWS * j, EXPERT_ROWS), :] = lo * xlo + hi * xhi
            return carry2

        lax.fori_loop(0, PAIRS // PAIR_CHUNK, chunk, 0)
        part = prod_ref[pl.ds(0, PAIRS, stride=EXPERT_ROWS), :]
        for s in range(1, EXPERT_ROWS):
            part = part + prod_ref[pl.ds(s, PAIRS, stride=EXPERT_ROWS), :]
        psum_ref[pl.ds(pl.multiple_of(t * PAIRS, PAIRS), PAIRS), :] = part
        return carry

    lax.fori_loop(0, tb, token, 0)
    ps = psum_ref[...]
    hi = ps.astype(BF16)
    lo = (ps - hi.astype(F32)).astype(BF16)
    ones = jnp.ones((8, 128), BF16)
    sums = _dot_nt(ones, hi) + _dot_nt(ones, lo)
    for t in range(tb):
        act_ref[t:t + 1, :] = sums[0:1, PAIRS * t:PAIRS * (t + 1)]
    wgt_ref[...] = gate_ref[...] * _gelu(act_ref[...])


REP_GROUP = 16


def _peer_v_kernel(idx_ref, wgt_ref, x1_ref, tbl_ref, out_ref, wrep_ref):
    tb = x1_ref.shape[0]
    eye = lax.broadcasted_iota(I32, (PAIRS, PAIRS), 0) == lax.broadcasted_iota(I32, (PAIRS, PAIRS), 1)
    ones = jnp.ones((PAIRS, 128), BF16)
    for g in range(tb // REP_GROUP):
        t0 = g * REP_GROUP
        d = jnp.concatenate([jnp.where(eye, wgt_ref[t:t + 1, :], 0.0) for t in range(t0, t0 + REP_GROUP)], axis=0)
        hi = d.astype(BF16)
        lo = (d - hi.astype(F32)).astype(BF16)
        wrep_ref[t0 * PAIRS:(t0 + REP_GROUP) * PAIRS, :] = _dot(hi, ones) + _dot(lo, ones)

    def token(t, carry):
        base = t * PAIRS

        def chunk(c, accs):
            off = base + c * PAIR_CHUNK
            accs = list(accs)
            for j in range(PAIR_CHUNK):
                lo, hi = _unpack(_expert(tbl_ref, idx_ref[off + j]))
                g = wrep_ref[pl.ds(off + j, 1), :]
                k = 2 * (j % 2)
                accs[k] = accs[k] + g * lo
                accs[k + 1] = accs[k + 1] + g * hi
            return tuple(accs)

        zero = jnp.zeros((EXPERT_ROWS, 128), F32)
        accs = lax.fori_loop(0, PAIRS // PAIR_CHUNK, chunk, (zero, zero, zero, zero))
        out_ref[t] = x1_ref[t] + jnp.concatenate([accs[0] + accs[2], accs[1] + accs[3]], axis=0)
        return carry

    lax.fori_loop(0, tb, token, 0)


def _table_spec(n_rows):
    return pl.BlockSpec((n_rows, 128), lambda i: (0, 0), pipeline_mode=pl.Buffered(1))


def _smem_spec(tb):
    return pl.BlockSpec((tb * PAIRS,), lambda i: (i,), memory_space=pltpu.SMEM)


def _stage_peer_u(idx, h2, gate, tbl, tb):
    n = h2.shape[0]
    tok = pl.BlockSpec((tb, 8, 128), lambda i: (i, 0, 0))
    row = pl.BlockSpec((tb, PAIRS), lambda i: (i, 0))
    return pl.pallas_call(
        _peer_u_kernel,
        grid=(n // tb,),
        in_specs=[_smem_spec(tb), tok, row, _table_spec(tbl.shape[0])],
        out_specs=row,
        out_shape=jax.ShapeDtypeStruct((n, PAIRS), F32),
        scratch_shapes=[pltpu.VMEM((PAIRS * EXPERT_ROWS, 128), F32), pltpu.VMEM((tb * PAIRS, 128), F32),
                        pltpu.VMEM((tb, PAIRS), F32)],
        compiler_params=_cparams("parallel"),
        name="peer_u",
    )(idx.reshape(-1), h2.reshape(n, 8, 128), gate, tbl)


def _stage_peer_v(idx, wgt, x1, tbl, tb):
    n = x1.shape[0]
    tok = pl.BlockSpec((tb, 8, 128), lambda i: (i, 0, 0))
    out = pl.pallas_call(
        _peer_v_kernel,
        grid=(n // tb,),
        in_specs=[_smem_spec(tb), pl.BlockSpec((tb, PAIRS), lambda i: (i, 0)), tok, _table_spec(tbl.shape[0])],
        out_specs=tok,
        out_shape=jax.ShapeDtypeStruct((n, 8, 128), F32),
        scratch_shapes=[pltpu.VMEM((tb * PAIRS, 128), F32)],
        compiler_params=_cparams("parallel"),
        name="peer_v",
    )(idx.reshape(-1), wgt, x1.reshape(n, 8, 128), tbl)
    return out.reshape(n, D_MODEL)


def _peer_block(x2, o_attn, y_b, g_attn_out, w_o, g_ffn, w_pq, sub_keys, expert_u, expert_v):
    n = x2.shape[0]
    x1, h2, qp = _stage_outproj(x2, o_attn, y_b, g_attn_out, w_o, g_ffn, w_pq, min(512, n))
    idx, gate = _stage_peer_route(qp, sub_keys, min(256, n))
    wgt = _stage_peer_u(idx, h2, gate, _pack_table(expert_u), min(128, n))
    return _stage_peer_v(idx, wgt, x1, _pack_table(expert_v), min(128, n))


def _mla_consts(g_cq, g_ckv, w_uq, w_uk, w_uv, g_qnorm, g_knorm):
    half = MLA_ROPE // 2
    inv = ROPE_THETA ** (-jnp.arange(half, dtype=F32) / half)
    z = lambda m: jnp.zeros((m,), F32)
    inv128 = jnp.concatenate([z(MLA_NOPE), inv, inv, z(HEAD_PAD - MLA_QK)]).reshape(1, HEAD_PAD)
    o = jnp.ones((half,), F32)
    m1 = jnp.concatenate([z(MLA_NOPE), -o, z(half), z(HEAD_PAD - MLA_QK)]).reshape(1, HEAD_PAD)
    m2 = jnp.concatenate([z(MLA_NOPE), z(half), o, z(HEAD_PAD - MLA_QK)]).reshape(1, HEAD_PAD)
    wuq = jnp.pad(w_uq.reshape(MLA_Q_RANK, MLA_HEADS, MLA_QK), ((0, 0), (0, 0), (0, HEAD_PAD - MLA_QK)))
    wuq = wuq.reshape(MLA_Q_RANK, MLA_HEADS * HEAD_PAD).astype(BF16)
    wuk = jnp.pad(w_uk.reshape(MLA_KV_RANK, MLA_HEADS, MLA_NOPE), ((0, 0), (0, 0), (0, HEAD_PAD - MLA_NOPE)))
    wuk = wuk.reshape(MLA_KV_RANK, MLA_HEADS * HEAD_PAD).astype(BF16)
    pad_g = lambda g: jnp.tile(jnp.pad(g, (0, HEAD_PAD - MLA_QK)), MLA_HEADS).reshape(1, MLA_HEADS * HEAD_PAD)
    bones = jnp.kron(jnp.eye(MLA_HEADS, dtype=F32), jnp.ones((HEAD_PAD, HEAD_PAD), F32)).astype(BF16)
    return (inv128, m1, m2, g_cq.reshape(1, -1), g_ckv.reshape(1, -1), wuq, wuk, w_uv.astype(BF16),
            pad_g(g_qnorm), pad_g(g_knorm), bones)


def _pad_w_in(w_in):
    z = jnp.zeros((D_MODEL, MLA_PAD - MLA_COLS), w_in.dtype)
    return jnp.concatenate([w_in[:, :MLA_COLS], z, w_in[:, MLA_COLS:]], axis=1).astype(BF16)


def _mla_branch(pm, positions, g_cq, g_ckv, w_uq, w_uk, w_uv, g_qnorm, g_knorm, batch, seq):
    n = batch * seq
    posf = positions.astype(F32).reshape(n, 1)
    consts = _mla_consts(g_cq, g_ckv, w_uq, w_uk, w_uv, g_qnorm, g_knorm)
    q, k, v = _stage_mla_prep(pm, posf, consts, min(512, n))
    return _stage_attn(q, k, v, batch, seq, min(1024, seq), min(1024, seq))


def kernel(x, positions, g_mix, w_in, rwkv_mu, g_cq, g_ckv, w_uq, w_uk, w_uv, g_qnorm, g_knorm, g_attn_out, w0, w2, a0, a2, g2, k_k, k_a, r_k, ln_x_w, ln_x_b, w_o, g_ffn, w_pq, sub_keys, expert_u, expert_v):
    batch, seq, _ = x.shape
    n = batch * seq
    x2 = x.reshape(n, D_MODEL)
    pm, pr = _stage_inproj(x2, g_mix[0], _pad_w_in(w_in[0]), min(512, n))
    o_attn = _mla_branch(pm, positions, g_cq[0], g_ckv[0], w_uq[0], w_uk[0], w_uv[0], g_qnorm[0], g_knorm[0],
                         batch, seq)
    y_b = _rwkv_branch(pr, rwkv_mu[0], w0[0], w2[0], a0[0], a2[0], g2[0], k_k[0], k_a[0], r_k[0], ln_x_w[0],
                       ln_x_b[0], batch, seq)
    out = _peer_block(x2, o_attn, y_b, g_attn_out[0], w_o[0], g_ffn[0], w_pq[0], sub_keys[0], expert_u[0],
                      expert_v[0])
    return out.reshape(batch, seq, D_MODEL)
```

```python
import functools

import jax
import jax.numpy as jnp
from jax import lax
from jax.experimental import pallas as pl
from jax.experimental.pallas import tpu as pltpu

F32 = jnp.float32
BF16 = jnp.bfloat16
I32 = jnp.int32

D_MODEL = 1024
CHUNK = 64
EPS = 1e-6

MLA_HEADS = 8
MLA_Q_RANK = 384
MLA_KV_RANK = 256
MLA_NOPE = 64
MLA_ROPE = 32
MLA_V = 64
MLA_QK = MLA_NOPE + MLA_ROPE
ROPE_THETA = 10000.0
MLA_COLS = MLA_Q_RANK + MLA_KV_RANK + MLA_ROPE
MLA_PAD = 768
HEAD_PAD = 128

RWKV_HEADS = 8
RWKV_HEAD = 64
RWKV_DIM = RWKV_HEADS * RWKV_HEAD
W_LORA = 64
A_LORA = 64
G_LORA = 128
LN_X_EPS = 64e-5
RWKV_COLS = 3 * RWKV_DIM + W_LORA + A_LORA + G_LORA

PEER_HEADS = 8
N_KEYS = 128
PEER_DQ = 256
PEER_DHALF = PEER_DQ // 2
PEER_TOPK = 16

VMEM_LIMIT = 48 * 1024 * 1024


def _cparams(*sem):
    return pltpu.CompilerParams(dimension_semantics=sem, vmem_limit_bytes=VMEM_LIMIT)


def _full(shape):
    n = len(shape)
    return pl.BlockSpec(shape, lambda *_: (0,) * n)


def _dot(a, b):
    return jnp.dot(a, b, preferred_element_type=F32)


def _dot_nt(a, b):
    return lax.dot_general(a, b, (((1,), (1,)), ((), ())), preferred_element_type=F32)


def _inproj_kernel(x_ref, g_ref, w_ref, om_ref, or_ref):
    x = x_ref[...]
    ms = jnp.mean(x * x, axis=-1, keepdims=True)
    h = (x * lax.rsqrt(ms + EPS) * g_ref[...]).astype(BF16)
    om_ref[...] = _dot(h, w_ref[:, :MLA_PAD])
    or_ref[...] = _dot(h, w_ref[:, MLA_PAD:])


def _stage_inproj(x2, g_mix, w_in_p, tm):
    n = x2.shape[0]
    return pl.pallas_call(
        _inproj_kernel,
        grid=(n // tm,),
        in_specs=[
            pl.BlockSpec((tm, D_MODEL), lambda i: (i, 0)),
            _full((1, D_MODEL)),
            _full((D_MODEL, MLA_PAD + RWKV_COLS)),
        ],
        out_specs=[
            pl.BlockSpec((tm, MLA_PAD), lambda i: (i, 0)),
            pl.BlockSpec((tm, RWKV_COLS), lambda i: (i, 0)),
        ],
        out_shape=[
            jax.ShapeDtypeStruct((n, MLA_PAD), F32),
            jax.ShapeDtypeStruct((n, RWKV_COLS), F32),
        ],
        compiler_params=_cparams("parallel"),
        name="inproj",
    )(x2, g_mix.reshape(1, D_MODEL), w_in_p)


def _mla_prep_kernel(pm_ref, pos_ref, inv_ref, m1_ref, m2_ref, gcq_ref, gckv_ref, wuq_ref, wuk_ref, wuv_ref,
                     gq_ref, gk_ref, bones_ref, q_ref, k_ref, v_ref):
    def rms(t, g):
        ms = jnp.mean(t * t, axis=-1, keepdims=True)
        return (t * lax.rsqrt(ms + EPS) * g).astype(BF16)

    cqn = rms(pm_ref[:, :MLA_Q_RANK], gcq_ref[...])
    ckvn = rms(pm_ref[:, MLA_Q_RANK:MLA_Q_RANK + MLA_KV_RANK], gckv_ref[...])
    q = _dot(cqn, wuq_ref[...])
    k = _dot(ckvn, wuk_ref[...])
    v_ref[...] = _dot(ckvn, wuv_ref[...]).astype(BF16)
    kr = pltpu.roll(pm_ref[:, MLA_Q_RANK + MLA_KV_RANK:], MLA_NOPE, axis=1)
    k = k + jnp.concatenate([kr] * MLA_HEADS, axis=1)

    ang = pos_ref[...] * inv_ref[...]
    cos = jnp.cos(ang)
    sin = jnp.sin(ang)
    c_all = jnp.concatenate([cos] * MLA_HEADS, axis=1)
    s1_all = jnp.concatenate([sin * m1_ref[...]] * MLA_HEADS, axis=1)
    s2_all = jnp.concatenate([sin * m2_ref[...]] * MLA_HEADS, axis=1)
    half = MLA_ROPE // 2
    width = MLA_HEADS * HEAD_PAD

    def norm_rope(t, g):
        ss = _dot((t * t).astype(BF16), bones_ref[...])
        t = t * lax.rsqrt(ss * (1.0 / MLA_QK) + EPS) * g
        return t * c_all + pltpu.roll(t, width - half, axis=1) * s1_all + pltpu.roll(t, half, axis=1) * s2_all

    q_ref[...] = (norm_rope(q, gq_ref[...]) * (MLA_QK ** -0.5)).astype(BF16)
    k_ref[...] = norm_rope(k, gk_ref[...]).astype(BF16)


def _stage_mla_prep(pm, posf, consts, tm):
    n = pm.shape[0]
    width = MLA_HEADS * HEAD_PAD
    vw = MLA_HEADS * MLA_V
    row = lambda w: pl.BlockSpec((tm, w), lambda i: (i, 0))
    return pl.pallas_call(
        _mla_prep_kernel,
        grid=(n // tm,),
        in_specs=[row(MLA_PAD), row(1)] + [_full(c.shape) for c in consts],
        out_specs=[row(width), row(width), row(vw)],
        out_shape=[
            jax.ShapeDtypeStruct((n, width), BF16),
            jax.ShapeDtypeStruct((n, width), BF16),
            jax.ShapeDtypeStruct((n, vw), BF16),
        ],
        compiler_params=_cparams("parallel"),
        name="mla_prep",
    )(pm, posf, *consts)


def _attn_kernel(q_ref, k_ref, v_ref, o_ref, *, tq, tk):
    qi = pl.program_id(2)
    sub = tq // tk
    qs = [q_ref[:, HEAD_PAD * j:HEAD_PAD * (j + 1)] for j in range(2)]

    def tile(kt, carry, mask):
        ks = pl.multiple_of(kt * tk, tk)
        vb = v_ref[pl.ds(ks, tk), :]
        out = []
        for j in range(2):
            m, l, acc = carry[j]
            s = _dot_nt(qs[j], k_ref[pl.ds(ks, tk), HEAD_PAD * j:HEAD_PAD * (j + 1)])
            if mask is not None:
                s = jnp.where(mask, s, -jnp.inf)
            m_new = jnp.maximum(m, jnp.max(s, axis=-1, keepdims=True))
            alpha = jnp.exp(m - m_new)
            p = jnp.exp(s - m_new)
            l = alpha * l + jnp.sum(p, axis=-1, keepdims=True)
            acc = alpha * acc + _dot(p.astype(BF16), vb)
            out.append((m_new, l, acc))
        return tuple(out)

    one = (jnp.full((tq, 1), -jnp.inf, F32), jnp.zeros((tq, 1), F32), jnp.zeros((tq, 2 * MLA_V), F32))
    carry = lax.fori_loop(0, qi * sub, functools.partial(tile, mask=None), (one, one))
    row = lax.broadcasted_iota(I32, (tq, tk), 0)
    col = lax.broadcasted_iota(I32, (tq, tk), 1)
    for d in range(sub):
        carry = tile(qi * sub + d, carry, ((col + d * tk) // CHUNK) <= (row // CHUNK))
    lane = lax.broadcasted_iota(I32, (tq, 2 * MLA_V), 1)
    o_ref[...] = jnp.where(lane < MLA_V, carry[0][2] / carry[0][1], carry[1][2] / carry[1][1])


def _stage_attn(q, k, v, batch, seq, tq, tk):
    n = q.shape[0]
    nq = seq // tq
    return pl.pallas_call(
        functools.partial(_attn_kernel, tq=tq, tk=tk),
        grid=(batch, MLA_HEADS // 2, nq),
        in_specs=[
            pl.BlockSpec((tq, 2 * HEAD_PAD), lambda b, h, i: (b * nq + i, h)),
            pl.BlockSpec((seq, 2 * HEAD_PAD), lambda b, h, i: (b, h)),
            pl.BlockSpec((seq, 2 * MLA_V), lambda b, h, i: (b, h)),
        ],
        out_specs=pl.BlockSpec((tq, 2 * MLA_V), lambda b, h, i: (b * nq + i, h)),
        out_shape=jax.ShapeDtypeStruct((n, MLA_HEADS * MLA_V), F32),
        compiler_params=_cparams("parallel", "parallel", "arbitrary"),
        name="attn",
    )(q, k, v)


def _dot_hi(a, b):
    return jnp.dot(a, b, preferred_element_type=F32, precision=lax.Precision.HIGHEST)


def _sigmoid(t):
    return 1.0 / (1.0 + jnp.exp(-t))


def _rwkv_prep_kernel(pr_ref, mu_ref, w0_ref, a0_ref, kk_ref, ka_ref, w2_ref, a2_ref, g2_ref, bones_ref,
                      r_o, k_o, v_o, al_o, be_o, lw_o, g_o, carry_ref, *, tiles_per_seq):
    i = pl.program_id(0)
    p = pr_ref[...]
    tm = p.shape[0]

    @pl.when(i % tiles_per_seq == 0)
    def _():
        carry_ref[...] = jnp.zeros_like(carry_ref)

    rowi = lax.broadcasted_iota(I32, p.shape, 0)
    prev = jnp.where(rowi == 0, carry_ref[0:1, :], pltpu.roll(p, 1, axis=0))
    carry_ref[0:1, :] = p[tm - 1:tm, :]
    xs = p + (prev - p) * mu_ref[...]
    d = RWKV_DIM
    r = xs[:, :d]
    k = xs[:, d:2 * d]
    v = xs[:, 2 * d:3 * d]
    z = xs[:, 3 * d:3 * d + W_LORA + A_LORA]
    gl = xs[:, 3 * d + W_LORA + A_LORA:]
    lane = lax.broadcasted_iota(I32, z.shape, 1)
    zt = jnp.where(lane < W_LORA, jnp.tanh(z), z)
    t = -(w0_ref[...] + _dot_hi(zt, w2_ref[...]))
    w = -(jnp.maximum(t, 0.0) + jnp.log1p(jnp.exp(-jnp.abs(t)))) - 0.5
    a = _sigmoid(a0_ref[...] + _dot_hi(zt, a2_ref[...]))
    kk = k * kk_ref[...]
    ss = _dot_hi(kk * kk, bones_ref[...])
    kk = kk / jnp.maximum(jnp.sqrt(ss), 1e-12)
    r_o[...] = r
    k_o[...] = k * (1.0 + (a - 1.0) * ka_ref[...])
    v_o[...] = v
    al_o[...] = -kk
    be_o[...] = kk * a
    lw_o[...] = -jnp.exp(w)
    g_o[...] = _dot(_sigmoid(gl).astype(BF16), g2_ref[...])


def _stage_rwkv_prep(pr, consts, seq, tm):
    n = pr.shape[0]
    d = RWKV_DIM
    row = lambda w: pl.BlockSpec((tm, w), lambda i: (i, 0))
    return pl.pallas_call(
        functools.partial(_rwkv_prep_kernel, tiles_per_seq=seq // tm),
        grid=(n // tm,),
        in_specs=[row(RWKV_COLS)] + [_full(c.shape) for c in consts],
        out_specs=[row(d)] * 7,
        out_shape=[jax.ShapeDtypeStruct((n, d), F32)] * 7,
        scratch_shapes=[pltpu.VMEM((8, RWKV_COLS), F32)],
        compiler_params=_cparams("arbitrary"),
        name="rwkv_prep",
    )(pr, *consts)


def _mm(a, b):
    return jnp.dot(a.astype(BF16), b.astype(BF16), preferred_element_type=F32)


def _mm_nt(a, b):
    return _dot_nt(a.astype(BF16), b.astype(BF16))


def _mm_tn(a, b):
    return lax.dot_general(a.astype(BF16), b.astype(BF16), (((0,), (0,)), ((), ())), preferred_element_type=F32)


SCAN_CHUNKS = 4


def _rwkv_scan_kernel(r_ref, k_ref, v_ref, al_ref, be_ref, lw_ref, g_ref, rk_ref, lnw_ref, lnb_ref, y_ref, s_ref):
    c = CHUNK
    hd = RWKV_HEAD

    @pl.when(pl.program_id(1) == 0)
    def _():
        s_ref[...] = jnp.zeros_like(s_ref)

    row = lax.broadcasted_iota(I32, (c, c), 0)
    col = lax.broadcasted_iota(I32, (c, c), 1)
    tri = jnp.where(col <= row, 1.0, 0.0).astype(F32)
    eye = jnp.where(col == row, 1.0, 0.0).astype(F32)
    row2 = lax.broadcasted_iota(I32, (2 * c, 2 * c), 0)
    col2 = lax.broadcasted_iota(I32, (2 * c, 2 * c), 1) % c
    keep = col2 < jnp.where(row2 < c, row2, row2 - c + 1)
    heads = range(RWKV_HEADS)
    units = [(h, g) for g in range(SCAN_CHUNKS) for h in heads]
    cat = lambda a, b: jnp.concatenate([a, b], axis=0)

    def lanes(h):
        return slice(hd * h, hd * (h + 1))

    def load(ref, u):
        return ref[c * u[1]:c * (u[1] + 1), lanes(u[0])]

    lw = {u: load(lw_ref, u) for u in units}
    cum = {u: _dot_hi(tri, lw[u]) for u in units}
    k = {u: load(k_ref, u) for u in units}
    v = {u: load(v_ref, u) for u in units}
    be = {u: load(be_ref, u) for u in units}
    rt = {u: load(r_ref, u) * jnp.exp(cum[u]) for u in units}
    at = {u: load(al_ref, u) * jnp.exp(cum[u] - lw[u]) for u in units}
    e_neg = {u: jnp.exp(-cum[u]) for u in units}
    m = {u: jnp.where(keep, _mm_nt(cat(at[u], rt[u]), cat(be[u] * e_neg[u], k[u] * e_neg[u])), 0.0) for u in units}
    m_top = {u: m[u][:c] for u in units}
    m_bot = {u: m[u][c:] for u in units}
    pw = {u: m_top[u][:, :c] for u in units}
    x = {u: eye + pw[u] for u in units}
    for _ in range(max(1, (c - 1).bit_length() - 1)):
        pw = {u: _mm(pw[u], pw[u]) for u in units}
        x = {u: x[u] + _mm(x[u], pw[u]) for u in units}
    w_mat = {u: _mm(x[u], at[u]) for u in units}
    akv = {u: _mm(m_top[u], cat(jnp.zeros_like(v[u]), v[u])) for u in units}
    u0 = {u: _mm(x[u], akv[u]) for u in units}
    cum_last = {u: cum[u][c - 1:c, :] for u in units}
    bk = {}
    for u in units:
        e_last = jnp.exp(cum_last[u] - cum[u])
        bk[u] = cat(be[u] * e_last, k[u] * e_last)

    s = [s_ref[h] for h in heads]
    for g in range(SCAN_CHUNKS):
        ws = [_mm_nt(cat(w_mat[h, g], rt[h, g]), s[h]) for h in heads]
        uv = [cat(ws[h][:c] + u0[h, g], v[h, g]) for h in heads]
        y = [ws[h][c:] + _mm(m_bot[h, g], uv[h]) for h in heads]
        s = [s[h] * jnp.exp(cum_last[h, g]) + _mm_tn(uv[h], bk[h, g]) for h in heads]
        for h in heads:
            u = (h, g)
            sl = lanes(h)
            mu = jnp.mean(y[h], axis=-1, keepdims=True)
            yc = y[h] - mu
            var = jnp.mean(yc * yc, axis=-1, keepdims=True)
            yn = yc * lax.rsqrt(var + LN_X_EPS) * lnw_ref[:, sl] + lnb_ref[:, sl]
            bonus = jnp.sum(load(r_ref, u) * k[u] * rk_ref[:, sl], axis=-1, keepdims=True) * v[u]
            y_ref[c * g:c * (g + 1), sl] = (yn + bonus) * load(g_ref, u)
    for h in heads:
        s_ref[h] = s[h]


def _stage_rwkv_scan(seqs, consts, batch, seq):
    n = batch * seq
    d = RWKV_DIM
    step = CHUNK * SCAN_CHUNKS
    nc = seq // step
    blk = pl.BlockSpec((step, d), lambda b, c: (b * nc + c, 0))
    return pl.pallas_call(
        _rwkv_scan_kernel,
        grid=(batch, nc),
        in_specs=[blk] * 7 + [_full((1, d))] * 3,
        out_specs=blk,
        out_shape=jax.ShapeDtypeStruct((n, d), F32),
        scratch_shapes=[pltpu.VMEM((RWKV_HEADS, RWKV_HEAD, RWKV_HEAD), F32)],
        compiler_params=_cparams("parallel", "arbitrary"),
        name="rwkv_scan",
    )(*seqs, *consts)


def _rwkv_branch(pr, rwkv_mu, w0, w2, a0, a2, g2, k_k, k_a, r_k, ln_x_w, ln_x_b, batch, seq):
    d = RWKV_DIM
    row = lambda t: t.reshape(1, -1)
    w2p = jnp.concatenate([w2, jnp.zeros((A_LORA, d), F32)], axis=0)
    a2p = jnp.concatenate([jnp.zeros((W_LORA, d), F32), a2], axis=0)
    bones = jnp.kron(jnp.eye(RWKV_HEADS, dtype=F32), jnp.ones((RWKV_HEAD, RWKV_HEAD), F32))
    consts = (row(rwkv_mu), row(w0), row(a0), row(k_k), row(k_a), w2p, a2p, g2.astype(BF16), bones)
    seqs = _stage_rwkv_prep(pr, consts, seq, min(512, seq))
    return _stage_rwkv_scan(seqs, (row(r_k), row(ln_x_w), row(ln_x_b)), batch, seq)


def _outproj_kernel(x_ref, oa_ref, yb_ref, gao_ref, bones_ref, wo_ref, gffn_ref, wpq_ref, x1_ref, h2_ref, qp_ref):
    oa = oa_ref[...]
    ss = _dot((oa * oa).astype(BF16), bones_ref[...])
    ya = oa * lax.rsqrt(ss * (1.0 / MLA_V) + EPS) * gao_ref[...]
    mix = jnp.concatenate([ya, yb_ref[...]], axis=1).astype(BF16)
    x1 = x_ref[...] + _dot(mix, wo_ref[...])
    x1_ref[...] = x1
    ms = jnp.mean(x1 * x1, axis=-1, keepdims=True)
    h2 = x1 * lax.rsqrt(ms + EPS) * gffn_ref[...]
    h2_ref[...] = h2
    qp_ref[...] = _dot(h2.astype(BF16), wpq_ref[...]).astype(BF16)


def _stage_outproj(x2, o_attn, y_b, g_attn_out, w_o, g_ffn, w_pq, tm):
    n = x2.shape[0]
    qw = PEER_HEADS * PEER_DQ
    aw = MLA_HEADS * MLA_V
    bones = jnp.kron(jnp.eye(MLA_HEADS, dtype=F32), jnp.ones((MLA_V, MLA_V), F32)).astype(BF16)
    row = lambda w: pl.BlockSpec((tm, w), lambda i: (i, 0))
    return pl.pallas_call(
        _outproj_kernel,
        grid=(n // tm,),
        in_specs=[row(D_MODEL), row(aw), row(RWKV_DIM), _full((1, aw)), _full((aw, aw)),
                  _full((D_MODEL, D_MODEL)), _full((1, D_MODEL)), _full((D_MODEL, qw))],
        out_specs=[row(D_MODEL), row(D_MODEL), row(qw)],
        out_shape=[
            jax.ShapeDtypeStruct((n, D_MODEL), F32),
            jax.ShapeDtypeStruct((n, D_MODEL), F32),
            jax.ShapeDtypeStruct((n, qw), BF16),
        ],
        compiler_params=_cparams("parallel"),
        name="outproj",
    )(x2, o_attn, y_b, g_attn_out.reshape(1, aw), bones, w_o.astype(BF16), g_ffn.reshape(1, D_MODEL),
      w_pq.astype(BF16))


def _topk_rows(s, payload=None):
    rows = s.shape[0]
    riota = lax.broadcasted_iota(I32, s.shape, 0)
    vals, ids = [], []
    for _ in range(PEER_TOPK):
        m = jnp.max(s, axis=0, keepdims=True)
        pos = jnp.min(jnp.where(s == m, riota, rows), axis=0, keepdims=True)
        sel = riota == pos
        vals.append(m)
        ids.append(pos if payload is None else jnp.max(jnp.where(sel, payload, -1), axis=0, keepdims=True))
        s = jnp.where(sel, -jnp.inf, s)
    return jnp.concatenate(vals, axis=0), jnp.concatenate(ids, axis=0)


def _peer_route_kernel(qp_ref, keys_ref, idx_ref, gate_ref, ids_t, gate_t):
    h = pl.program_id(1)
    s1 = _dot_nt(keys_ref[0], qp_ref[:, :PEER_DHALF])
    s2 = _dot_nt(keys_ref[1], qp_ref[:, PEER_DHALF:])
    v1, i1 = _topk_rows(s1)
    v2, i2 = _topk_rows(s2)
    sub8 = lax.broadcasted_iota(I32, (8, s1.shape[1]), 0)
    i1 = i1 * (N_KEYS * EXPERT_ROWS)
    i2 = i2 * EXPERT_ROWS
    blocks_s = [v1[0:1] + v2]
    blocks_i = [i1[0:1] + i2]
    for i in range(1, 8):
        valid = sub8 < PEER_TOPK // (i + 1)
        blocks_s.append(jnp.where(valid, v1[i:i + 1] + v2[:8], -jnp.inf))
        blocks_i.append(i1[i:i + 1] + i2[:8])
    blocks_s.append(v1[8:] + v2[0:1])
    blocks_i.append(i1[8:] + i2[0:1])
    top_s, ids = _topk_rows(jnp.concatenate(blocks_s, axis=0), jnp.concatenate(blocks_i, axis=0))
    e = jnp.exp(top_s - top_s[0:1])
    gate = e / jnp.sum(e, axis=0, keepdims=True)
    r0 = pl.multiple_of(h * PEER_TOPK, PEER_TOPK)
    ids_t[pl.ds(r0, PEER_TOPK), :] = ids
    gate_t[pl.ds(r0, PEER_TOPK), :] = gate

    @pl.when(h == PEER_HEADS - 1)
    def _():
        idx_ref[...] = ids_t[...].T
        gate_ref[...] = gate_t[...].T


def _stage_peer_route(qp, sub_keys, tt):
    n = qp.shape[0]
    kw = PEER_HEADS * PEER_TOPK
    keys = sub_keys.reshape(PEER_HEADS * 2, N_KEYS, PEER_DHALF).astype(BF16)
    out = pl.BlockSpec((tt, kw), lambda i, h: (i, 0))
    return pl.pallas_call(
        _peer_route_kernel,
        grid=(n // tt, PEER_HEADS),
        in_specs=[
            pl.BlockSpec((tt, PEER_DQ), lambda i, h: (i, h)),
            pl.BlockSpec((2, N_KEYS, PEER_DHALF), lambda i, h: (h, 0, 0)),
        ],
        out_specs=[out, out],
        out_shape=[jax.ShapeDtypeStruct((n, kw), I32), jax.ShapeDtypeStruct((n, kw), F32)],
        scratch_shapes=[pltpu.VMEM((kw, tt), I32), pltpu.VMEM((kw, tt), F32)],
        compiler_params=_cparams("parallel", "arbitrary"),
        name="peer_route",
    )(qp, keys)


EXPERT_ROWS = 4
PAIRS = PEER_HEADS * PEER_TOPK


def _pack_table(t):
    tb = t.astype(BF16).reshape(t.shape[0], EXPERT_ROWS, 2, 128).transpose(0, 1, 3, 2)
    return lax.bitcast_convert_type(tb, I32).reshape(t.shape[0] * EXPERT_ROWS, 128)


def _expert(tbl_ref, row):
    w = tbl_ref[pl.ds(pl.multiple_of(row, EXPERT_ROWS), EXPERT_ROWS), :]
    return pltpu.bitcast(w, BF16).astype(F32)


def _gelu(t):
    return 0.5 * t * (1.0 + lax.erf(t * (2.0 ** -0.5)))


TOKEN_UNROLL = 2
V_PAIR_CHUNK = 64
FOLD_ROWS = 4


def _peer_u_kernel(idx_ref, x_ref, gate_ref, tbl_ref, wgt_ref, prod_ref, psum_ref, act_ref):
    tb = x_ref.shape[0]

    def tokens(i, carry):
        for k in range(TOKEN_UNROLL):
            t = i * TOKEN_UNROLL + k
            xt = x_ref[t]
            base = t * PAIRS
            for p in range(PAIRS):
                prod = _expert(tbl_ref, idx_ref[base + p]) * xt
                prod_ref[k, FOLD_ROWS * p:FOLD_ROWS * (p + 1), :] = prod[:FOLD_ROWS] + prod[FOLD_ROWS:]
            part = prod_ref[k, pl.ds(0, PAIRS, stride=FOLD_ROWS), :]
            for s in range(1, FOLD_ROWS):
                part = part + prod_ref[k, pl.ds(s, PAIRS, stride=FOLD_ROWS), :]
            psum_ref[pl.ds(pl.multiple_of(t * PAIRS, PAIRS), PAIRS), :] = part
        return carry

    lax.fori_loop(0, tb // TOKEN_UNROLL, tokens, 0)
    ps = psum_ref[...]
    hi = ps.astype(BF16)
    lo = (ps - hi.astype(F32)).astype(BF16)
    ones = jnp.ones((8, 128), BF16)
    sums = _dot_nt(ones, hi) + _dot_nt(ones, lo)
    for t in range(tb):
        act_ref[t:t + 1, :] = sums[0:1, PAIRS * t:PAIRS * (t + 1)]
    wgt_ref[...] = gate_ref[...] * _gelu(act_ref[...])


REP_GROUP = 16


def _peer_v_kernel(idx_ref, wgt_ref, x1_ref, tbl_ref, out_ref, wrep_ref):
    tb = x1_ref.shape[0]
    eye = lax.broadcasted_iota(I32, (PAIRS, PAIRS), 0) == lax.broadcasted_iota(I32, (PAIRS, PAIRS), 1)
    ones = jnp.ones((PAIRS, 128), BF16)
    for g in range(tb // REP_GROUP):
        t0 = g * REP_GROUP
        d = jnp.concatenate([jnp.where(eye, wgt_ref[t:t + 1, :], 0.0) for t in range(t0, t0 + REP_GROUP)], axis=0)
        wrep_ref[t0 * PAIRS:(t0 + REP_GROUP) * PAIRS, :] = _dot(d.astype(BF16), ones)

    def token(t, carry):
        base = t * PAIRS

        def chunk(c, accs):
            off = base + c * V_PAIR_CHUNK
            accs = list(accs)
            for j in range(V_PAIR_CHUNK):
                accs[j % 4] = accs[j % 4] + wrep_ref[pl.ds(off + j, 1), :] * _expert(tbl_ref, idx_ref[off + j])
            return tuple(accs)

        zero = jnp.zeros((8, 128), F32)
        accs = lax.fori_loop(0, PAIRS // V_PAIR_CHUNK, chunk, (zero,) * 4)
        out_ref[t] = x1_ref[t] + ((accs[0] + accs[1]) + (accs[2] + accs[3]))
        return carry

    lax.fori_loop(0, tb, token, 0)


def _table_spec(n_rows):
    return pl.BlockSpec((n_rows, 128), lambda i: (0, 0), pipeline_mode=pl.Buffered(1))


def _smem_spec(tb):
    return pl.BlockSpec((tb * PAIRS,), lambda i: (i,), memory_space=pltpu.SMEM)


def _stage_peer_u(idx, h2, gate, tbl, tb):
    n = h2.shape[0]
    tok = pl.BlockSpec((tb, 8, 128), lambda i: (i, 0, 0))
    row = pl.BlockSpec((tb, PAIRS), lambda i: (i, 0))
    return pl.pallas_call(
        _peer_u_kernel,
        grid=(n // tb,),
        in_specs=[_smem_spec(tb), tok, row, _table_spec(tbl.shape[0])],
        out_specs=row,
        out_shape=jax.ShapeDtypeStruct((n, PAIRS), F32),
        scratch_shapes=[pltpu.VMEM((TOKEN_UNROLL, PAIRS * FOLD_ROWS, 128), F32),
                        pltpu.VMEM((tb * PAIRS, 128), F32),
                        pltpu.VMEM((tb, PAIRS), F32)],
        compiler_params=_cparams("parallel"),
        name="peer_u",
    )(idx.reshape(-1), h2.reshape(n, 8, 128), gate, tbl)


def _stage_peer_v(idx, wgt, x1, tbl, tb):
    n = x1.shape[0]
    tok = pl.BlockSpec((tb, 8, 128), lambda i: (i, 0, 0))
    out = pl.pallas_call(
        _peer_v_kernel,
        grid=(n // tb,),
        in_specs=[_smem_spec(tb), pl.BlockSpec((tb, PAIRS), lambda i: (i, 0)), tok, _table_spec(tbl.shape[0])],
        out_specs=tok,
        out_shape=jax.ShapeDtypeStruct((n, 8, 128), F32),
        scratch_shapes=[pltpu.VMEM((tb * PAIRS, 128), F32)],
        compiler_params=_cparams("parallel"),
        name="peer_v",
    )(idx.reshape(-1), wgt, x1.reshape(n, 8, 128), tbl)
    return out.reshape(n, D_MODEL)


def _peer_block(x2, o_attn, y_b, g_attn_out, w_o, g_ffn, w_pq, sub_keys, expert_u, expert_v):
    n = x2.shape[0]
    x1, h2, qp = _stage_outproj(x2, o_attn, y_b, g_attn_out, w_o, g_ffn, w_pq, min(512, n))
    idx, gate = _stage_peer_route(qp, sub_keys, min(256, n))
    wgt = _stage_peer_u(idx, h2, gate, _pack_table(expert_u), min(128, n))
    return _stage_peer_v(idx, wgt, x1, _pack_table(expert_v), min(128, n))


def _mla_consts(g_cq, g_ckv, w_uq, w_uk, w_uv, g_qnorm, g_knorm):
    half = MLA_ROPE // 2
    inv = ROPE_THETA ** (-jnp.arange(half, dtype=F32) / half)
    z = lambda m: jnp.zeros((m,), F32)
    inv128 = jnp.concatenate([z(MLA_NOPE), inv, inv, z(HEAD_PAD - MLA_QK)]).reshape(1, HEAD_PAD)
    o = jnp.ones((half,), F32)
    m1 = jnp.concatenate([z(MLA_NOPE), -o, z(half), z(HEAD_PAD - MLA_QK)]).reshape(1, HEAD_PAD)
    m2 = jnp.concatenate([z(MLA_NOPE), z(half), o, z(HEAD_PAD - MLA_QK)]).reshape(1, HEAD_PAD)
    wuq = jnp.pad(w_uq.reshape(MLA_Q_RANK, MLA_HEADS, MLA_QK), ((0, 0), (0, 0), (0, HEAD_PAD - MLA_QK)))
    wuq = wuq.reshape(MLA_Q_RANK, MLA_HEADS * HEAD_PAD).astype(BF16)
    wuk = jnp.pad(w_uk.reshape(MLA_KV_RANK, MLA_HEADS, MLA_NOPE), ((0, 0), (0, 0), (0, HEAD_PAD - MLA_NOPE)))
    wuk = wuk.reshape(MLA_KV_RANK, MLA_HEADS * HEAD_PAD).astype(BF16)
    pad_g = lambda g: jnp.tile(jnp.pad(g, (0, HEAD_PAD - MLA_QK)), MLA_HEADS).reshape(1, MLA_HEADS * HEAD_PAD)
    bones = jnp.kron(jnp.eye(MLA_HEADS, dtype=F32), jnp.ones((HEAD_PAD, HEAD_PAD), F32)).astype(BF16)
    return (inv128, m1, m2, g_cq.reshape(1, -1), g_ckv.reshape(1, -1), wuq, wuk, w_uv.astype(BF16),
            pad_g(g_qnorm), pad_g(g_knorm), bones)


def _pad_w_in(w_in):
    z = jnp.zeros((D_MODEL, MLA_PAD - MLA_COLS), w_in.dtype)
    return jnp.concatenate([w_in[:, :MLA_COLS], z, w_in[:, MLA_COLS:]], axis=1).astype(BF16)


def _mla_branch(pm, positions, g_cq, g_ckv, w_uq, w_uk, w_uv, g_qnorm, g_knorm, batch, seq):
    n = batch * seq
    posf = positions.astype(F32).reshape(n, 1)
    consts = _mla_consts(g_cq, g_ckv, w_uq, w_uk, w_uv, g_qnorm, g_knorm)
    q, k, v = _stage_mla_prep(pm, posf, consts, min(512, n))
    return _stage_attn(q, k, v, batch, seq, min(1024, seq), min(1024, seq))


def kernel(x, positions, g_mix, w_in, rwkv_mu, g_cq, g_ckv, w_uq, w_uk, w_uv, g_qnorm, g_knorm, g_attn_out, w0, w2, a0, a2, g2, k_k, k_a, r_k, ln_x_w, ln_x_b, w_o, g_ffn, w_pq, sub_keys, expert_u, expert_v):
    batch, seq, _ = x.shape
    n = batch * seq
    x2 = x.reshape(n, D_MODEL)
    pm, pr = _stage_inproj(x2, g_mix[0], _pad_w_in(w_in[0]), min(512, n))
    o_attn = _mla_branch(pm, positions, g_cq[0], g_ckv[0], w_uq[0], w_uk[0], w_uv[0], g_qnorm[0], g_knorm[0],
                         batch, seq)
    y_b = _rwkv_branch(pr, rwkv_mu[0], w0[0], w2[0], a0[0], a2[0], g2[0], k_k[0], k_a[0], r_k[0], ln_x_w[0],
                       ln_x_b[0], batch, seq)
    out = _peer_block(x2, o_attn, y_b, g_attn_out[0], w_o[0], g_ffn[0], w_pq[0], sub_keys[0], expert_u[0],
                      expert_v[0])
    return out.reshape(batch, seq, D_MODEL)
```

```python
import functools

import jax
import jax.numpy as jnp
from jax import lax
from jax.experimental import pallas as pl
from jax.experimental.pallas import tpu as pltpu

F32 = jnp.float32
BF16 = jnp.bfloat16
I32 = jnp.int32

D_MODEL = 1024
CHUNK = 64
EPS = 1e-6

MLA_HEADS = 8
MLA_Q_RANK = 384
MLA_KV_RANK = 256
MLA_NOPE = 64
MLA_ROPE = 32
MLA_V = 64
MLA_QK = MLA_NOPE + MLA_ROPE
ROPE_THETA = 10000.0
MLA_COLS = MLA_Q_RANK + MLA_KV_RANK + MLA_ROPE
MLA_PAD = 768
HEAD_PAD = 128

RWKV_HEADS = 8
RWKV_HEAD = 64
RWKV_DIM = RWKV_HEADS * RWKV_HEAD
W_LORA = 64
A_LORA = 64
G_LORA = 128
LN_X_EPS = 64e-5
RWKV_COLS = 3 * RWKV_DIM + W_LORA + A_LORA + G_LORA

PEER_HEADS = 8
N_KEYS = 128
PEER_DQ = 256
PEER_DHALF = PEER_DQ // 2
PEER_TOPK = 16

VMEM_LIMIT = 48 * 1024 * 1024


def _cparams(*sem):
    return pltpu.CompilerParams(dimension_semantics=sem, vmem_limit_bytes=VMEM_LIMIT)


def _full(shape):
    n = len(shape)
    return pl.BlockSpec(shape, lambda *_: (0,) * n)


def _dot(a, b):
    return jnp.dot(a, b, preferred_element_type=F32)


def _dot_nt(a, b):
    return lax.dot_general(a, b, (((1,), (1,)), ((), ())), preferred_element_type=F32)


def _inproj_kernel(x_ref, g_ref, w_ref, om_ref, or_ref):
    x = x_ref[...]
    ms = jnp.mean(x * x, axis=-1, keepdims=True)
    h = (x * lax.rsqrt(ms + EPS) * g_ref[...]).astype(BF16)
    om_ref[...] = _dot(h, w_ref[:, :MLA_PAD])
    or_ref[...] = _dot(h, w_ref[:, MLA_PAD:])


def _stage_inproj(x2, g_mix, w_in_p, tm):
    n = x2.shape[0]
    return pl.pallas_call(
        _inproj_kernel,
        grid=(n // tm,),
        in_specs=[
            pl.BlockSpec((tm, D_MODEL), lambda i: (i, 0)),
            _full((1, D_MODEL)),
            _full((D_MODEL, MLA_PAD + RWKV_COLS)),
        ],
        out_specs=[
            pl.BlockSpec((tm, MLA_PAD), lambda i: (i, 0)),
            pl.BlockSpec((tm, RWKV_COLS), lambda i: (i, 0)),
        ],
        out_shape=[
            jax.ShapeDtypeStruct((n, MLA_PAD), F32),
            jax.ShapeDtypeStruct((n, RWKV_COLS), F32),
        ],
        compiler_params=_cparams("parallel"),
        name="inproj",
    )(x2, g_mix.reshape(1, D_MODEL), w_in_p)


def _mla_prep_kernel(pm_ref, pos_ref, inv_ref, m1_ref, m2_ref, gcq_ref, gckv_ref, wuq_ref, wuk_ref, wuv_ref,
                     gq_ref, gk_ref, bones_ref, q_ref, k_ref, v_ref):
    def rms(t, g):
        ms = jnp.mean(t * t, axis=-1, keepdims=True)
        return (t * lax.rsqrt(ms + EPS) * g).astype(BF16)

    cqn = rms(pm_ref[:, :MLA_Q_RANK], gcq_ref[...])
    ckvn = rms(pm_ref[:, MLA_Q_RANK:MLA_Q_RANK + MLA_KV_RANK], gckv_ref[...])
    q = _dot(cqn, wuq_ref[...])
    k = _dot(ckvn, wuk_ref[...])
    v_ref[...] = _dot(ckvn, wuv_ref[...]).astype(BF16)
    kr = pltpu.roll(pm_ref[:, MLA_Q_RANK + MLA_KV_RANK:], MLA_NOPE, axis=1)
    k = k + jnp.concatenate([kr] * MLA_HEADS, axis=1)

    ang = pos_ref[...] * inv_ref[...]
    cos = jnp.cos(ang)
    sin = jnp.sin(ang)
    c_all = jnp.concatenate([cos] * MLA_HEADS, axis=1)
    s1_all = jnp.concatenate([sin * m1_ref[...]] * MLA_HEADS, axis=1)
    s2_all = jnp.concatenate([sin * m2_ref[...]] * MLA_HEADS, axis=1)
    half = MLA_ROPE // 2
    width = MLA_HEADS * HEAD_PAD

    def norm_rope(t, g):
        ss = _dot((t * t).astype(BF16), bones_ref[...])
        t = t * lax.rsqrt(ss * (1.0 / MLA_QK) + EPS) * g
        return t * c_all + pltpu.roll(t, width - half, axis=1) * s1_all + pltpu.roll(t, half, axis=1) * s2_all

    q_ref[...] = (norm_rope(q, gq_ref[...]) * (MLA_QK ** -0.5)).astype(BF16)
    k_ref[...] = norm_rope(k, gk_ref[...]).astype(BF16)


def _stage_mla_prep(pm, posf, consts, tm):
    n = pm.shape[0]
    width = MLA_HEADS * HEAD_PAD
    vw = MLA_HEADS * MLA_V
    row = lambda w: pl.BlockSpec((tm, w), lambda i: (i, 0))
    return pl.pallas_call(
        _mla_prep_kernel,
        grid=(n // tm,),
        in_specs=[row(MLA_PAD), row(1)] + [_full(c.shape) for c in consts],
        out_specs=[row(width), row(width), row(vw)],
        out_shape=[
            jax.ShapeDtypeStruct((n, width), BF16),
            jax.ShapeDtypeStruct((n, width), BF16),
            jax.ShapeDtypeStruct((n, vw), BF16),
        ],
        compiler_params=_cparams("parallel"),
        name="mla_prep",
    )(pm, posf, *consts)


def _attn_kernel(q_ref, k_ref, v_ref, o_ref, *, tq, tk):
    qi = pl.program_id(2)
    sub = tq // tk
    qs = [q_ref[:, HEAD_PAD * j:HEAD_PAD * (j + 1)] for j in range(2)]

    def tile(kt, carry, mask):
        ks = pl.multiple_of(kt * tk, tk)
        vb = v_ref[pl.ds(ks, tk), :]
        out = []
        for j in range(2):
            m, l, acc = carry[j]
            s = _dot_nt(qs[j], k_ref[pl.ds(ks, tk), HEAD_PAD * j:HEAD_PAD * (j + 1)])
            if mask is not None:
                s = jnp.where(mask, s, -jnp.inf)
            m_new = jnp.maximum(m, jnp.max(s, axis=-1, keepdims=True))
            alpha = jnp.exp(m - m_new)
            p = jnp.exp(s - m_new)
            l = alpha * l + jnp.sum(p, axis=-1, keepdims=True)
            acc = alpha * acc + _dot(p.astype(BF16), vb)
            out.append((m_new, l, acc))
        return tuple(out)

    one = (jnp.full((tq, 1), -jnp.inf, F32), jnp.zeros((tq, 1), F32), jnp.zeros((tq, 2 * MLA_V), F32))
    carry = lax.fori_loop(0, qi * sub, functools.partial(tile, mask=None), (one, one))
    row = lax.broadcasted_iota(I32, (tq, tk), 0)
    col = lax.broadcasted_iota(I32, (tq, tk), 1)
    for d in range(sub):
        carry = tile(qi * sub + d, carry, ((col + d * tk) // CHUNK) <= (row // CHUNK))
    lane = lax.broadcasted_iota(I32, (tq, 2 * MLA_V), 1)
    o_ref[...] = jnp.where(lane < MLA_V, carry[0][2] / carry[0][1], carry[1][2] / carry[1][1])


def _stage_attn(q, k, v, batch, seq, tq, tk):
    n = q.shape[0]
    nq = seq // tq
    return pl.pallas_call(
        functools.partial(_attn_kernel, tq=tq, tk=tk),
        grid=(batch, MLA_HEADS // 2, nq),
        in_specs=[
            pl.BlockSpec((tq, 2 * HEAD_PAD), lambda b, h, i: (b * nq + i, h)),
            pl.BlockSpec((seq, 2 * HEAD_PAD), lambda b, h, i: (b, h)),
            pl.BlockSpec((seq, 2 * MLA_V), lambda b, h, i: (b, h)),
        ],
        out_specs=pl.BlockSpec((tq, 2 * MLA_V), lambda b, h, i: (b * nq + i, h)),
        out_shape=jax.ShapeDtypeStruct((n, MLA_HEADS * MLA_V), F32),
        compiler_params=_cparams("parallel", "parallel", "arbitrary"),
        name="attn",
    )(q, k, v)


def _dot_hi(a, b):
    return jnp.dot(a, b, preferred_element_type=F32, precision=lax.Precision.HIGHEST)


def _sigmoid(t):
    return 1.0 / (1.0 + jnp.exp(-t))


def _rwkv_prep_kernel(pr_ref, mu_ref, w0_ref, a0_ref, kk_ref, ka_ref, w2_ref, a2_ref, g2_ref, bones_ref,
                      r_o, k_o, v_o, al_o, be_o, lw_o, g_o, carry_ref, *, tiles_per_seq):
    i = pl.program_id(0)
    p = pr_ref[...]
    tm = p.shape[0]

    @pl.when(i % tiles_per_seq == 0)
    def _():
        carry_ref[...] = jnp.zeros_like(carry_ref)

    rowi = lax.broadcasted_iota(I32, p.shape, 0)
    prev = jnp.where(rowi == 0, carry_ref[0:1, :], pltpu.roll(p, 1, axis=0))
    carry_ref[0:1, :] = p[tm - 1:tm, :]
    xs = p + (prev - p) * mu_ref[...]
    d = RWKV_DIM
    r = xs[:, :d]
    k = xs[:, d:2 * d]
    v = xs[:, 2 * d:3 * d]
    z = xs[:, 3 * d:3 * d + W_LORA + A_LORA]
    gl = xs[:, 3 * d + W_LORA + A_LORA:]
    lane = lax.broadcasted_iota(I32, z.shape, 1)
    zt = jnp.where(lane < W_LORA, jnp.tanh(z), z)
    t = -(w0_ref[...] + _dot_hi(zt, w2_ref[...]))
    w = -(jnp.maximum(t, 0.0) + jnp.log1p(jnp.exp(-jnp.abs(t)))) - 0.5
    a = _sigmoid(a0_ref[...] + _dot_hi(zt, a2_ref[...]))
    kk = k * kk_ref[...]
    ss = _dot_hi(kk * kk, bones_ref[...])
    kk = kk / jnp.maximum(jnp.sqrt(ss), 1e-12)
    r_o[...] = r
    k_o[...] = k * (1.0 + (a - 1.0) * ka_ref[...])
    v_o[...] = v
    al_o[...] = -kk
    be_o[...] = kk * a
    lw_o[...] = -jnp.exp(w)
    g_o[...] = _dot(_sigmoid(gl).astype(BF16), g2_ref[...])


def _stage_rwkv_prep(pr, consts, seq, tm):
    n = pr.shape[0]
    d = RWKV_DIM
    row = lambda w: pl.BlockSpec((tm, w), lambda i: (i, 0))
    return pl.pallas_call(
        functools.partial(_rwkv_prep_kernel, tiles_per_seq=seq // tm),
        grid=(n // tm,),
        in_specs=[row(RWKV_COLS)] + [_full(c.shape) for c in consts],
        out_specs=[row(d)] * 7,
        out_shape=[jax.ShapeDtypeStruct((n, d), F32)] * 7,
        scratch_shapes=[pltpu.VMEM((8, RWKV_COLS), F32)],
        compiler_params=_cparams("arbitrary"),
        name="rwkv_prep",
    )(pr, *consts)


def _mm(a, b):
    return jnp.dot(a.astype(BF16), b.astype(BF16), preferred_element_type=F32)


def _mm_nt(a, b):
    return _dot_nt(a.astype(BF16), b.astype(BF16))


def _mm_tn(a, b):
    return lax.dot_general(a.astype(BF16), b.astype(BF16), (((0,), (0,)), ((), ())), preferred_element_type=F32)


SCAN_CHUNKS = 4


def _rwkv_scan_kernel(r_ref, k_ref, v_ref, al_ref, be_ref, lw_ref, g_ref, rk_ref, lnw_ref, lnb_ref, y_ref, s_ref):
    c = CHUNK
    hd = RWKV_HEAD

    @pl.when(pl.program_id(1) == 0)
    def _():
        s_ref[...] = jnp.zeros_like(s_ref)

    row = lax.broadcasted_iota(I32, (c, c), 0)
    col = lax.broadcasted_iota(I32, (c, c), 1)
    tri = jnp.where(col <= row, 1.0, 0.0).astype(F32)
    eye = jnp.where(col == row, 1.0, 0.0).astype(F32)
    row2 = lax.broadcasted_iota(I32, (2 * c, 2 * c), 0)
    col2 = lax.broadcasted_iota(I32, (2 * c, 2 * c), 1) % c
    keep = col2 < jnp.where(row2 < c, row2, row2 - c + 1)
    heads = range(RWKV_HEADS)
    units = [(h, g) for g in range(SCAN_CHUNKS) for h in heads]
    cat = lambda a, b: jnp.concatenate([a, b], axis=0)

    def lanes(h):
        return slice(hd * h, hd * (h + 1))

    def load(ref, u):
        return ref[c * u[1]:c * (u[1] + 1), lanes(u[0])]

    lw = {u: load(lw_ref, u) for u in units}
    cum = {u: _dot_hi(tri, lw[u]) for u in units}
    k = {u: load(k_ref, u) for u in units}
    v = {u: load(v_ref, u) for u in units}
    be = {u: load(be_ref, u) for u in units}
    rt = {u: load(r_ref, u) * jnp.exp(cum[u]) for u in units}
    at = {u: load(al_ref, u) * jnp.exp(cum[u] - lw[u]) for u in units}
    e_neg = {u: jnp.exp(-cum[u]) for u in units}
    m = {u: jnp.where(keep, _mm_nt(cat(at[u], rt[u]), cat(be[u] * e_neg[u], k[u] * e_neg[u])), 0.0) for u in units}
    m_top = {u: m[u][:c] for u in units}
    m_bot = {u: m[u][c:] for u in units}
    pw = {u: m_top[u][:, :c] for u in units}
    x = {u: eye + pw[u] for u in units}
    for _ in range(max(1, (c - 1).bit_length() - 1)):
        pw = {u: _mm(pw[u], pw[u]) for u in units}
        x = {u: x[u] + _mm(x[u], pw[u]) for u in units}
    w_mat = {u: _mm(x[u], at[u]) for u in units}
    akv = {u: _mm(m_top[u], cat(jnp.zeros_like(v[u]), v[u])) for u in units}
    u0 = {u: _mm(x[u], akv[u]) for u in units}
    cum_last = {u: cum[u][c - 1:c, :] for u in units}
    bk = {}
    for u in units:
        e_last = jnp.exp(cum_last[u] - cum[u])
        bk[u] = cat(be[u] * e_last, k[u] * e_last)

    s = [s_ref[h] for h in heads]
    for g in range(SCAN_CHUNKS):
        ws = [_mm_nt(cat(w_mat[h, g], rt[h, g]), s[h]) for h in heads]
        uv = [cat(ws[h][:c] + u0[h, g], v[h, g]) for h in heads]
        y = [ws[h][c:] + _mm(m_bot[h, g], uv[h]) for h in heads]
        s = [s[h] * jnp.exp(cum_last[h, g]) + _mm_tn(uv[h], bk[h, g]) for h in heads]
        for h in heads:
            u = (h, g)
            sl = lanes(h)
            mu = jnp.mean(y[h], axis=-1, keepdims=True)
            yc = y[h] - mu
            var = jnp.mean(yc * yc, axis=-1, keepdims=True)
            yn = yc * lax.rsqrt(var + LN_X_EPS) * lnw_ref[:, sl] + lnb_ref[:, sl]
            bonus = jnp.sum(load(r_ref, u) * k[u] * rk_ref[:, sl], axis=-1, keepdims=True) * v[u]
            y_ref[c * g:c * (g + 1), sl] = (yn + bonus) * load(g_ref, u)
    for h in heads:
        s_ref[h] = s[h]


def _stage_rwkv_scan(seqs, consts, batch, seq):
    n = batch * seq
    d = RWKV_DIM
    step = CHUNK * SCAN_CHUNKS
    nc = seq // step
    blk = pl.BlockSpec((step, d), lambda b, c: (b * nc + c, 0))
    return pl.pallas_call(
        _rwkv_scan_kernel,
        grid=(batch, nc),
        in_specs=[blk] * 7 + [_full((1, d))] * 3,
        out_specs=blk,
        out_shape=jax.ShapeDtypeStruct((n, d), F32),
        scratch_shapes=[pltpu.VMEM((RWKV_HEADS, RWKV_HEAD, RWKV_HEAD), F32)],
        compiler_params=_cparams("parallel", "arbitrary"),
        name="rwkv_scan",
    )(*seqs, *consts)


def _rwkv_branch(pr, rwkv_mu, w0, w2, a0, a2, g2, k_k, k_a, r_k, ln_x_w, ln_x_b, batch, seq):
    d = RWKV_DIM
    row = lambda t: t.reshape(1, -1)
    w2p = jnp.concatenate([w2, jnp.zeros((A_LORA, d), F32)], axis=0)
    a2p = jnp.concatenate([jnp.zeros((W_LORA, d), F32), a2], axis=0)
    bones = jnp.kron(jnp.eye(RWKV_HEADS, dtype=F32), jnp.ones((RWKV_HEAD, RWKV_HEAD), F32))
    consts = (row(rwkv_mu), row(w0), row(a0), row(k_k), row(k_a), w2p, a2p, g2.astype(BF16), bones)
    seqs = _stage_rwkv_prep(pr, consts, seq, min(512, seq))
    return _stage_rwkv_scan(seqs, (row(r_k), row(ln_x_w), row(ln_x_b)), batch, seq)


def _outproj_kernel(x_ref, oa_ref, yb_ref, gao_ref, bones_ref, wo_ref, gffn_ref, wpq_ref, x1_ref, h2_ref, qp_ref):
    oa = oa_ref[...]
    ss = _dot((oa * oa).astype(BF16), bones_ref[...])
    ya = oa * lax.rsqrt(ss * (1.0 / MLA_V) + EPS) * gao_ref[...]
    mix = jnp.concatenate([ya, yb_ref[...]], axis=1).astype(BF16)
    x1 = x_ref[...] + _dot(mix, wo_ref[...])
    x1_ref[...] = x1
    ms = jnp.mean(x1 * x1, axis=-1, keepdims=True)
    h2 = x1 * lax.rsqrt(ms + EPS) * gffn_ref[...]
    h2_ref[...] = h2
    qp_ref[...] = _dot(h2.astype(BF16), wpq_ref[...]).astype(BF16)


def _stage_outproj(x2, o_attn, y_b, g_attn_out, w_o, g_ffn, w_pq, tm):
    n = x2.shape[0]
    qw = PEER_HEADS * PEER_DQ
    aw = MLA_HEADS * MLA_V
    bones = jnp.kron(jnp.eye(MLA_HEADS, dtype=F32), jnp.ones((MLA_V, MLA_V), F32)).astype(BF16)
    row = lambda w: pl.BlockSpec((tm, w), lambda i: (i, 0))
    return pl.pallas_call(
        _outproj_kernel,
        grid=(n // tm,),
        in_specs=[row(D_MODEL), row(aw), row(RWKV_DIM), _full((1, aw)), _full((aw, aw)),
                  _full((D_MODEL, D_MODEL)), _full((1, D_MODEL)), _full((D_MODEL, qw))],
        out_specs=[row(D_MODEL), row(D_MODEL), row(qw)],
        out_shape=[
            jax.ShapeDtypeStruct((n, D_MODEL), F32),
            jax.ShapeDtypeStruct((n, D_MODEL), F32),
            jax.ShapeDtypeStruct((n, qw), BF16),
        ],
        compiler_params=_cparams("parallel"),
        name="outproj",
    )(x2, o_attn, y_b, g_attn_out.reshape(1, aw), bones, w_o.astype(BF16), g_ffn.reshape(1, D_MODEL),
      w_pq.astype(BF16))


def _topk_rows(s, payload=None):
    rows = s.shape[0]
    riota = lax.broadcasted_iota(I32, s.shape, 0)
    vals, ids = [], []
    for _ in range(PEER_TOPK):
        m = jnp.max(s, axis=0, keepdims=True)
        pos = jnp.min(jnp.where(s == m, riota, rows), axis=0, keepdims=True)
        sel = riota == pos
        vals.append(m)
        ids.append(pos if payload is None else jnp.max(jnp.where(sel, payload, -1), axis=0, keepdims=True))
        s = jnp.where(sel, -jnp.inf, s)
    return jnp.concatenate(vals, axis=0), jnp.concatenate(ids, axis=0)


def _peer_route_kernel(qp_ref, keys_ref, idx_ref, gate_ref, ids_t, gate_t):
    h = pl.program_id(1)
    s1 = _dot_nt(keys_ref[0], qp_ref[:, :PEER_DHALF])
    s2 = _dot_nt(keys_ref[1], qp_ref[:, PEER_DHALF:])
    v1, i1 = _topk_rows(s1)
    v2, i2 = _topk_rows(s2)
    sub8 = lax.broadcasted_iota(I32, (8, s1.shape[1]), 0)
    i1 = i1 * (N_KEYS * EXPERT_ROWS)
    i2 = i2 * EXPERT_ROWS
    blocks_s = [v1[0:1] + v2]
    blocks_i = [i1[0:1] + i2]
    for i in range(1, 8):
        valid = sub8 < PEER_TOPK // (i + 1)
        blocks_s.append(jnp.where(valid, v1[i:i + 1] + v2[:8], -jnp.inf))
        blocks_i.append(i1[i:i + 1] + i2[:8])
    blocks_s.append(v1[8:] + v2[0:1])
    blocks_i.append(i1[8:] + i2[0:1])
    top_s, ids = _topk_rows(jnp.concatenate(blocks_s, axis=0), jnp.concatenate(blocks_i, axis=0))
    e = jnp.exp(top_s - top_s[0:1])
    gate = e / jnp.sum(e, axis=0, keepdims=True)
    r0 = pl.multiple_of(h * PEER_TOPK, PEER_TOPK)
    ids_t[pl.ds(r0, PEER_TOPK), :] = ids
    gate_t[pl.ds(r0, PEER_TOPK), :] = gate

    @pl.when(h == PEER_HEADS - 1)
    def _():
        idx_ref[...] = ids_t[...].T
        gate_ref[...] = gate_t[...].T


def _stage_peer_route(qp, sub_keys, tt):
    n = qp.shape[0]
    kw = PEER_HEADS * PEER_TOPK
    keys = sub_keys.reshape(PEER_HEADS * 2, N_KEYS, PEER_DHALF).astype(BF16)
    out = pl.BlockSpec((tt, kw), lambda i, h: (i, 0))
    return pl.pallas_call(
        _peer_route_kernel,
        grid=(n // tt, PEER_HEADS),
        in_specs=[
            pl.BlockSpec((tt, PEER_DQ), lambda i, h: (i, h)),
            pl.BlockSpec((2, N_KEYS, PEER_DHALF), lambda i, h: (h, 0, 0)),
        ],
        out_specs=[out, out],
        out_shape=[jax.ShapeDtypeStruct((n, kw), I32), jax.ShapeDtypeStruct((n, kw), F32)],
        scratch_shapes=[pltpu.VMEM((kw, tt), I32), pltpu.VMEM((kw, tt), F32)],
        compiler_params=_cparams("parallel", "arbitrary"),
        name="peer_route",
    )(qp, keys)


EXPERT_ROWS = 4
PAIRS = PEER_HEADS * PEER_TOPK


PACK_EXPERTS = 256


def _pack_kernel(t_ref, o_ref):
    rows = t_ref[...].reshape(PACK_EXPERTS * 8, 128)
    o_ref[...] = pltpu.bitcast(rows.astype(BF16), I32)


def _pack_table(t):
    n_exp = t.shape[0]
    return pl.pallas_call(
        _pack_kernel,
        grid=(n_exp // PACK_EXPERTS,),
        in_specs=[pl.BlockSpec((PACK_EXPERTS, D_MODEL), lambda i: (i, 0))],
        out_specs=pl.BlockSpec((PACK_EXPERTS * EXPERT_ROWS, 128), lambda i: (i, 0)),
        out_shape=jax.ShapeDtypeStruct((n_exp * EXPERT_ROWS, 128), I32),
        compiler_params=_cparams("parallel"),
        name="pack_table",
    )(t)


def _expert(tbl_ref, row):
    w = tbl_ref[pl.ds(pl.multiple_of(row, EXPERT_ROWS), EXPERT_ROWS), :]
    return pltpu.bitcast(w, BF16).astype(F32)


def _gelu(t):
    return 0.5 * t * (1.0 + lax.erf(t * (2.0 ** -0.5)))


TOKEN_UNROLL = 4
V_ACCS = 2
FOLD_ROWS = 4


def _peer_u_kernel(idx_ref, x_ref, gate_ref, tbl_ref, wgt_ref, prod_ref, psum_ref, act_ref):
    tb = x_ref.shape[0]

    def tokens(i, carry):
        for k in range(TOKEN_UNROLL):
            t = i * TOKEN_UNROLL + k
            xt = x_ref[pl.ds(t, 1), :].reshape(8, 128)
            base = t * PAIRS
            for p in range(PAIRS):
                prod = _expert(tbl_ref, idx_ref[base + p]) * xt
                prod_ref[k, FOLD_ROWS * p:FOLD_ROWS * (p + 1), :] = prod[:FOLD_ROWS] + prod[FOLD_ROWS:]
            part = prod_ref[k, pl.ds(0, PAIRS, stride=FOLD_ROWS), :]
            for s in range(1, FOLD_ROWS):
                part = part + prod_ref[k, pl.ds(s, PAIRS, stride=FOLD_ROWS), :]
            psum_ref[pl.ds(pl.multiple_of(t * PAIRS, PAIRS), PAIRS), :] = part
        return carry

    lax.fori_loop(0, tb // TOKEN_UNROLL, tokens, 0)
    ps = psum_ref[...]
    hi = ps.astype(BF16)
    lo = (ps - hi.astype(F32)).astype(BF16)
    ones = jnp.ones((8, 128), BF16)
    sums = _dot_nt(ones, hi) + _dot_nt(ones, lo)
    for t in range(tb):
        act_ref[t:t + 1, :] = sums[0:1, PAIRS * t:PAIRS * (t + 1)]
    wgt_ref[...] = gate_ref[...] * _gelu(act_ref[...])


REP_GROUP = 16


def _peer_v_kernel(idx_ref, wgt_ref, x1_ref, tbl_ref, out_ref, wrep_ref):
    tb = x1_ref.shape[0]
    eye = lax.broadcasted_iota(I32, (PAIRS, PAIRS), 0) == lax.broadcasted_iota(I32, (PAIRS, PAIRS), 1)
    ones = jnp.ones((PAIRS, 128), BF16)
    for g in range(tb // REP_GROUP):
        t0 = g * REP_GROUP
        d = jnp.concatenate([jnp.where(eye, wgt_ref[t:t + 1, :], 0.0) for t in range(t0, t0 + REP_GROUP)], axis=0)
        wrep_ref[t0 * PAIRS:(t0 + REP_GROUP) * PAIRS, :] = _dot(d.astype(BF16), ones)

    def token(t, carry):
        base = t * PAIRS
        accs = [jnp.zeros((8, 128), F32) for _ in range(V_ACCS)]
        for p in range(PAIRS):
            accs[p % V_ACCS] = (accs[p % V_ACCS]
                                + wrep_ref[pl.ds(base + p, 1), :] * _expert(tbl_ref, idx_ref[base + p]))
        out = x1_ref[pl.ds(t, 1), :].reshape(8, 128) + (accs[0] + accs[1])
        out_ref[pl.ds(t, 1), :] = out.reshape(1, D_MODEL)
        return carry

    lax.fori_loop(0, tb, token, 0)


def _table_spec(n_rows):
    return pl.BlockSpec((n_rows, 128), lambda i: (0, 0), pipeline_mode=pl.Buffered(1))


def _smem_spec(tb):
    return pl.BlockSpec((tb * PAIRS,), lambda i: (i,), memory_space=pltpu.SMEM)


def _stage_peer_u(idx, h2, gate, tbl, tb):
    n = h2.shape[0]
    tok = pl.BlockSpec((tb, D_MODEL), lambda i: (i, 0))
    row = pl.BlockSpec((tb, PAIRS), lambda i: (i, 0))
    return pl.pallas_call(
        _peer_u_kernel,
        grid=(n // tb,),
        in_specs=[_smem_spec(tb), tok, row, _table_spec(tbl.shape[0])],
        out_specs=row,
        out_shape=jax.ShapeDtypeStruct((n, PAIRS), F32),
        scratch_shapes=[pltpu.VMEM((TOKEN_UNROLL, PAIRS * FOLD_ROWS, 128), F32),
                        pltpu.VMEM((tb * PAIRS, 128), F32),
                        pltpu.VMEM((tb, PAIRS), F32)],
        compiler_params=_cparams("parallel"),
        name="peer_u",
    )(idx.reshape(-1), h2, gate, tbl)


def _stage_peer_v(idx, wgt, x1, tbl, tb):
    n = x1.shape[0]
    tok = pl.BlockSpec((tb, D_MODEL), lambda i: (i, 0))
    return pl.pallas_call(
        _peer_v_kernel,
        grid=(n // tb,),
        in_specs=[_smem_spec(tb), pl.BlockSpec((tb, PAIRS), lambda i: (i, 0)), tok, _table_spec(tbl.shape[0])],
        out_specs=tok,
        out_shape=jax.ShapeDtypeStruct((n, D_MODEL), F32),
        scratch_shapes=[pltpu.VMEM((tb * PAIRS, 128), F32)],
        compiler_params=_cparams("parallel"),
        name="peer_v",
    )(idx.reshape(-1), wgt, x1, tbl)


def _peer_block(x2, o_attn, y_b, g_attn_out, w_o, g_ffn, w_pq, sub_keys, expert_u, expert_v):
    n = x2.shape[0]
    x1, h2, qp = _stage_outproj(x2, o_attn, y_b, g_attn_out, w_o, g_ffn, w_pq, min(512, n))
    idx, gate = _stage_peer_route(qp, sub_keys, min(256, n))
    wgt = _stage_peer_u(idx, h2, gate, _pack_table(expert_u), min(128, n))
    return _stage_peer_v(idx, wgt, x1, _pack_table(expert_v), min(128, n))


def _mla_consts(g_cq, g_ckv, w_uq, w_uk, w_uv, g_qnorm, g_knorm):
    half = MLA_ROPE // 2
    inv = ROPE_THETA ** (-jnp.arange(half, dtype=F32) / half)
    z = lambda m: jnp.zeros((m,), F32)
    inv128 = jnp.concatenate([z(MLA_NOPE), inv, inv, z(HEAD_PAD - MLA_QK)]).reshape(1, HEAD_PAD)
    o = jnp.ones((half,), F32)
    m1 = jnp.concatenate([z(MLA_NOPE), -o, z(half), z(HEAD_PAD - MLA_QK)]).reshape(1, HEAD_PAD)
    m2 = jnp.concatenate([z(MLA_NOPE), z(half), o, z(HEAD_PAD - MLA_QK)]).reshape(1, HEAD_PAD)
    wuq = jnp.pad(w_uq.reshape(MLA_Q_RANK, MLA_HEADS, MLA_QK), ((0, 0), (0, 0), (0, HEAD_PAD - MLA_QK)))
    wuq = wuq.reshape(MLA_Q_RANK, MLA_HEADS * HEAD_PAD).astype(BF16)
    wuk = jnp.pad(w_uk.reshape(MLA_KV_RANK, MLA_HEADS, MLA_NOPE), ((0, 0), (0, 0), (0, HEAD_PAD - MLA_NOPE)))
    wuk = wuk.reshape(MLA_KV_RANK, MLA_HEADS * HEAD_PAD).astype(BF16)
    pad_g = lambda g: jnp.tile(jnp.pad(g, (0, HEAD_PAD - MLA_QK)), MLA_HEADS).reshape(1, MLA_HEADS * HEAD_PAD)
    bones = jnp.kron(jnp.eye(MLA_HEADS, dtype=F32), jnp.ones((HEAD_PAD, HEAD_PAD), F32)).astype(BF16)
    return (inv128, m1, m2, g_cq.reshape(1, -1), g_ckv.reshape(1, -1), wuq, wuk, w_uv.astype(BF16),
            pad_g(g_qnorm), pad_g(g_knorm), bones)


def _pad_w_in(w_in):
    z = jnp.zeros((D_MODEL, MLA_PAD - MLA_COLS), w_in.dtype)
    return jnp.concatenate([w_in[:, :MLA_COLS], z, w_in[:, MLA_COLS:]], axis=1).astype(BF16)


def _mla_branch(pm, positions, g_cq, g_ckv, w_uq, w_uk, w_uv, g_qnorm, g_knorm, batch, seq):
    n = batch * seq
    posf = positions.astype(F32).reshape(n, 1)
    consts = _mla_consts(g_cq, g_ckv, w_uq, w_uk, w_uv, g_qnorm, g_knorm)
    q, k, v = _stage_mla_prep(pm, posf, consts, min(512, n))
    return _stage_attn(q, k, v, batch, seq, min(1024, seq), min(1024, seq))


def kernel(x, positions, g_mix, w_in, rwkv_mu, g_cq, g_ckv, w_uq, w_uk, w_uv, g_qnorm, g_knorm, g_attn_out, w0, w2, a0, a2, g2, k_k, k_a, r_k, ln_x_w, ln_x_b, w_o, g_ffn, w_pq, sub_keys, expert_u, expert_v):
    batch, seq, _ = x.shape
    n = batch * seq
    x2 = x.reshape(n, D_MODEL)
    pm, pr = _stage_inproj(x2, g_mix[0], _pad_w_in(w_in[0]), min(512, n))
    o_attn = _mla_branch(pm, positions, g_cq[0], g_ckv[0], w_uq[0], w_uk[0], w_uv[0], g_qnorm[0], g_knorm[0],
                         batch, seq)
    y_b = _rwkv_branch(pr, rwkv_mu[0], w0[0], w2[0], a0[0], a2[0], g2[0], k_k[0], k_a[0], r_k[0], ln_x_w[0],
                       ln_x_b[0], batch, seq)
    out = _peer_block(x2, o_attn, y_b, g_attn_out[0], w_o[0], g_ffn[0], w_pq[0], sub_keys[0], expert_u[0],
                      expert_v[0])
    return out.reshape(batch, seq, D_MODEL)
```

```python
import functools

import jax
import jax.numpy as jnp
from jax import lax
from jax.experimental import pallas as pl
from jax.experimental.pallas import tpu as pltpu

F32 = jnp.float32
BF16 = jnp.bfloat16
I32 = jnp.int32

D_MODEL = 1024
CHUNK = 64
EPS = 1e-6

MLA_HEADS = 8
MLA_Q_RANK = 384
MLA_KV_RANK = 256
MLA_NOPE = 64
MLA_ROPE = 32
MLA_V = 64
MLA_QK = MLA_NOPE + MLA_ROPE
ROPE_THETA = 10000.0
MLA_COLS = MLA_Q_RANK + MLA_KV_RANK + MLA_ROPE
MLA_PAD = 768
HEAD_PAD = 128

RWKV_HEADS = 8
RWKV_HEAD = 64
RWKV_DIM = RWKV_HEADS * RWKV_HEAD
W_LORA = 64
A_LORA = 64
G_LORA = 128
LN_X_EPS = 64e-5
RWKV_COLS = 3 * RWKV_DIM + W_LORA + A_LORA + G_LORA

PEER_HEADS = 8
N_KEYS = 128
PEER_DQ = 256
PEER_DHALF = PEER_DQ // 2
PEER_TOPK = 16

V7X_VMEM_BYTES = 64 * 1024 * 1024
VMEM_LIMIT = V7X_VMEM_BYTES * 3 // 4


def _cparams(*sem):
    return pltpu.CompilerParams(dimension_semantics=sem, vmem_limit_bytes=VMEM_LIMIT)


def _full(shape):
    n = len(shape)
    return pl.BlockSpec(shape, lambda *_: (0,) * n)


def _dot(a, b):
    return jnp.dot(a, b, preferred_element_type=F32)


def _dot_nt(a, b):
    return lax.dot_general(a, b, (((1,), (1,)), ((), ())), preferred_element_type=F32)


def _inproj_kernel(x_ref, g_ref, w_ref, om_ref, or_ref):
    x = x_ref[...]
    ms = jnp.mean(x * x, axis=-1, keepdims=True)
    h = (x * lax.rsqrt(ms + EPS) * g_ref[...]).astype(BF16)
    om_ref[...] = _dot(h, w_ref[:, :MLA_PAD])
    or_ref[...] = _dot(h, w_ref[:, MLA_PAD:])


def _stage_inproj(x2, g_mix, w_in_p, tm):
    n = x2.shape[0]
    return pl.pallas_call(
        _inproj_kernel,
        grid=(n // tm,),
        in_specs=[
            pl.BlockSpec((tm, D_MODEL), lambda i: (i, 0)),
            _full((1, D_MODEL)),
            _full((D_MODEL, MLA_PAD + RWKV_COLS)),
        ],
        out_specs=[
            pl.BlockSpec((tm, MLA_PAD), lambda i: (i, 0)),
            pl.BlockSpec((tm, RWKV_COLS), lambda i: (i, 0)),
        ],
        out_shape=[
            jax.ShapeDtypeStruct((n, MLA_PAD), F32),
            jax.ShapeDtypeStruct((n, RWKV_COLS), F32),
        ],
        compiler_params=_cparams("parallel"),
        name="inproj",
    )(x2, g_mix.reshape(1, D_MODEL), w_in_p)


def _mla_prep_kernel(pm_ref, pos_ref, inv_ref, m1_ref, m2_ref, gcq_ref, gckv_ref, wuq_ref, wuk_ref, wuv_ref,
                     gq_ref, gk_ref, bones_ref, q_ref, k_ref, v_ref):
    def rms(t, g):
        ms = jnp.mean(t * t, axis=-1, keepdims=True)
        return (t * lax.rsqrt(ms + EPS) * g).astype(BF16)

    cqn = rms(pm_ref[:, :MLA_Q_RANK], gcq_ref[...])
    ckvn = rms(pm_ref[:, MLA_Q_RANK:MLA_Q_RANK + MLA_KV_RANK], gckv_ref[...])
    q = _dot(cqn, wuq_ref[...])
    k = _dot(ckvn, wuk_ref[...])
    v_ref[...] = _dot(ckvn, wuv_ref[...]).astype(BF16)
    kr = pltpu.roll(pm_ref[:, MLA_Q_RANK + MLA_KV_RANK:], MLA_NOPE, axis=1)
    k = k + jnp.concatenate([kr] * MLA_HEADS, axis=1)

    ang = pos_ref[...] * inv_ref[...]
    cos = jnp.cos(ang)
    sin = jnp.sin(ang)
    c_all = jnp.concatenate([cos] * MLA_HEADS, axis=1)
    s1_all = jnp.concatenate([sin * m1_ref[...]] * MLA_HEADS, axis=1)
    s2_all = jnp.concatenate([sin * m2_ref[...]] * MLA_HEADS, axis=1)
    half = MLA_ROPE // 2
    width = MLA_HEADS * HEAD_PAD

    def norm_rope(t, g):
        ss = _dot((t * t).astype(BF16), bones_ref[...])
        t = t * lax.rsqrt(ss * (1.0 / MLA_QK) + EPS) * g
        return t * c_all + pltpu.roll(t, width - half, axis=1) * s1_all + pltpu.roll(t, half, axis=1) * s2_all

    q_ref[...] = (norm_rope(q, gq_ref[...]) * (MLA_QK ** -0.5)).astype(BF16)
    k_ref[...] = norm_rope(k, gk_ref[...]).astype(BF16)


def _stage_mla_prep(pm, posf, consts, tm):
    n = pm.shape[0]
    width = MLA_HEADS * HEAD_PAD
    vw = MLA_HEADS * MLA_V
    row = lambda w: pl.BlockSpec((tm, w), lambda i: (i, 0))
    return pl.pallas_call(
        _mla_prep_kernel,
        grid=(n // tm,),
        in_specs=[row(MLA_PAD), row(1)] + [_full(c.shape) for c in consts],
        out_specs=[row(width), row(width), row(vw)],
        out_shape=[
            jax.ShapeDtypeStruct((n, width), BF16),
            jax.ShapeDtypeStruct((n, width), BF16),
            jax.ShapeDtypeStruct((n, vw), BF16),
        ],
        compiler_params=_cparams("parallel"),
        name="mla_prep",
    )(pm, posf, *consts)


def _attn_kernel(q_ref, k_ref, v_ref, o_ref, *, tq, tk):
    qi = pl.program_id(2)
    sub = tq // tk
    qs = [q_ref[:, HEAD_PAD * j:HEAD_PAD * (j + 1)] for j in range(2)]

    def tile(kt, carry, mask):
        ks = pl.multiple_of(kt * tk, tk)
        vb = v_ref[pl.ds(ks, tk), :]
        out = []
        for j in range(2):
            m, l, acc = carry[j]
            s = _dot_nt(qs[j], k_ref[pl.ds(ks, tk), HEAD_PAD * j:HEAD_PAD * (j + 1)])
            if mask is not None:
                s = jnp.where(mask, s, -jnp.inf)
            m_new = jnp.maximum(m, jnp.max(s, axis=-1, keepdims=True))
            alpha = jnp.exp(m - m_new)
            p = jnp.exp(s - m_new)
            l = alpha * l + jnp.sum(p, axis=-1, keepdims=True)
            acc = alpha * acc + _dot(p.astype(BF16), vb)
            out.append((m_new, l, acc))
        return tuple(out)

    one = (jnp.full((tq, 1), -jnp.inf, F32), jnp.zeros((tq, 1), F32), jnp.zeros((tq, 2 * MLA_V), F32))
    carry = lax.fori_loop(0, qi * sub, functools.partial(tile, mask=None), (one, one))
    row = lax.broadcasted_iota(I32, (tq, tk), 0)
    col = lax.broadcasted_iota(I32, (tq, tk), 1)
    for d in range(sub):
        carry = tile(qi * sub + d, carry, ((col + d * tk) // CHUNK) <= (row // CHUNK))
    lane = lax.broadcasted_iota(I32, (tq, 2 * MLA_V), 1)
    o_ref[...] = jnp.where(lane < MLA_V, carry[0][2] / carry[0][1], carry[1][2] / carry[1][1])


def _stage_attn(q, k, v, batch, seq, tq, tk):
    n = q.shape[0]
    nq = seq // tq
    return pl.pallas_call(
        functools.partial(_attn_kernel, tq=tq, tk=tk),
        grid=(batch, MLA_HEADS // 2, nq),
        in_specs=[
            pl.BlockSpec((tq, 2 * HEAD_PAD), lambda b, h, i: (b * nq + i, h)),
            pl.BlockSpec((seq, 2 * HEAD_PAD), lambda b, h, i: (b, h)),
            pl.BlockSpec((seq, 2 * MLA_V), lambda b, h, i: (b, h)),
        ],
        out_specs=pl.BlockSpec((tq, 2 * MLA_V), lambda b, h, i: (b * nq + i, h)),
        out_shape=jax.ShapeDtypeStruct((n, MLA_HEADS * MLA_V), F32),
        compiler_params=_cparams("parallel", "parallel", "arbitrary"),
        name="attn",
    )(q, k, v)


def _dot_hi(a, b):
    return jnp.dot(a, b, preferred_element_type=F32, precision=lax.Precision.HIGHEST)


def _sigmoid(t):
    return 1.0 / (1.0 + jnp.exp(-t))


def _rwkv_prep_kernel(pr_ref, mu_ref, w0_ref, a0_ref, kk_ref, ka_ref, w2_ref, a2_ref, g2_ref, bones_ref,
                      r_o, k_o, v_o, al_o, be_o, lw_o, g_o, carry_ref, *, tiles_per_seq):
    i = pl.program_id(0)
    p = pr_ref[...]
    tm = p.shape[0]

    @pl.when(i % tiles_per_seq == 0)
    def _():
        carry_ref[...] = jnp.zeros_like(carry_ref)

    rowi = lax.broadcasted_iota(I32, p.shape, 0)
    prev = jnp.where(rowi == 0, carry_ref[0:1, :], pltpu.roll(p, 1, axis=0))
    carry_ref[0:1, :] = p[tm - 1:tm, :]
    xs = p + (prev - p) * mu_ref[...]
    d = RWKV_DIM
    r = xs[:, :d]
    k = xs[:, d:2 * d]
    v = xs[:, 2 * d:3 * d]
    z = xs[:, 3 * d:3 * d + W_LORA + A_LORA]
    gl = xs[:, 3 * d + W_LORA + A_LORA:]
    lane = lax.broadcasted_iota(I32, z.shape, 1)
    zt = jnp.where(lane < W_LORA, jnp.tanh(z), z)
    t = -(w0_ref[...] + _dot_hi(zt, w2_ref[...]))
    w = -(jnp.maximum(t, 0.0) + jnp.log1p(jnp.exp(-jnp.abs(t)))) - 0.5
    a = _sigmoid(a0_ref[...] + _dot_hi(zt, a2_ref[...]))
    kk = k * kk_ref[...]
    ss = _dot_hi(kk * kk, bones_ref[...])
    kk = kk / jnp.maximum(jnp.sqrt(ss), 1e-12)
    r_o[...] = r
    k_o[...] = k * (1.0 + (a - 1.0) * ka_ref[...])
    v_o[...] = v
    al_o[...] = -kk
    be_o[...] = kk * a
    lw_o[...] = -jnp.exp(w)
    g_o[...] = _dot(_sigmoid(gl).astype(BF16), g2_ref[...])


def _stage_rwkv_prep(pr, consts, seq, tm):
    n = pr.shape[0]
    d = RWKV_DIM
    row = lambda w: pl.BlockSpec((tm, w), lambda i: (i, 0))
    return pl.pallas_call(
        functools.partial(_rwkv_prep_kernel, tiles_per_seq=seq // tm),
        grid=(n // tm,),
        in_specs=[row(RWKV_COLS)] + [_full(c.shape) for c in consts],
        out_specs=[row(d)] * 7,
        out_shape=[jax.ShapeDtypeStruct((n, d), F32)] * 7,
        scratch_shapes=[pltpu.VMEM((8, RWKV_COLS), F32)],
        compiler_params=_cparams("arbitrary"),
        name="rwkv_prep",
    )(pr, *consts)


def _mm(a, b):
    return jnp.dot(a.astype(BF16), b.astype(BF16), preferred_element_type=F32)


def _mm_nt(a, b):
    return _dot_nt(a.astype(BF16), b.astype(BF16))


def _mm_tn(a, b):
    return lax.dot_general(a.astype(BF16), b.astype(BF16), (((0,), (0,)), ((), ())), preferred_element_type=F32)


SCAN_CHUNKS = 4


def _rwkv_scan_kernel(r_ref, k_ref, v_ref, al_ref, be_ref, lw_ref, g_ref, rk_ref, lnw_ref, lnb_ref, y_ref, s_ref):
    c = CHUNK
    hd = RWKV_HEAD

    @pl.when(pl.program_id(1) == 0)
    def _():
        s_ref[...] = jnp.zeros_like(s_ref)

    row = lax.broadcasted_iota(I32, (c, c), 0)
    col = lax.broadcasted_iota(I32, (c, c), 1)
    tri = jnp.where(col <= row, 1.0, 0.0).astype(F32)
    eye = jnp.where(col == row, 1.0, 0.0).astype(F32)
    row2 = lax.broadcasted_iota(I32, (2 * c, 2 * c), 0)
    col2 = lax.broadcasted_iota(I32, (2 * c, 2 * c), 1) % c
    keep = col2 < jnp.where(row2 < c, row2, row2 - c + 1)
    heads = range(RWKV_HEADS)
    units = [(h, g) for g in range(SCAN_CHUNKS) for h in heads]
    cat = lambda a, b: jnp.concatenate([a, b], axis=0)

    def lanes(h):
        return slice(hd * h, hd * (h + 1))

    def load(ref, u):
        return ref[c * u[1]:c * (u[1] + 1), lanes(u[0])]

    lw = {u: load(lw_ref, u) for u in units}
    cum = {u: _dot_hi(tri, lw[u]) for u in units}
    k = {u: load(k_ref, u) for u in units}
    v = {u: load(v_ref, u) for u in units}
    be = {u: load(be_ref, u) for u in units}
    rt = {u: load(r_ref, u) * jnp.exp(cum[u]) for u in units}
    at = {u: load(al_ref, u) * jnp.exp(cum[u] - lw[u]) for u in units}
    e_neg = {u: jnp.exp(-cum[u]) for u in units}
    m = {u: jnp.where(keep, _mm_nt(cat(at[u], rt[u]), cat(be[u] * e_neg[u], k[u] * e_neg[u])), 0.0) for u in units}
    m_top = {u: m[u][:c] for u in units}
    m_bot = {u: m[u][c:] for u in units}
    pw = {u: m_top[u][:, :c] for u in units}
    x = {u: eye + pw[u] for u in units}
    for _ in range(max(1, (c - 1).bit_length() - 1)):
        pw = {u: _mm(pw[u], pw[u]) for u in units}
        x = {u: x[u] + _mm(x[u], pw[u]) for u in units}
    w_mat = {u: _mm(x[u], at[u]) for u in units}
    akv = {u: _mm(m_top[u], cat(jnp.zeros_like(v[u]), v[u])) for u in units}
    u0 = {u: _mm(x[u], akv[u]) for u in units}
    cum_last = {u: cum[u][c - 1:c, :] for u in units}
    bk = {}
    for u in units:
        e_last = jnp.exp(cum_last[u] - cum[u])
        bk[u] = cat(be[u] * e_last, k[u] * e_last)

    s = [s_ref[h] for h in heads]
    for g in range(SCAN_CHUNKS):
        ws = [_mm_nt(cat(w_mat[h, g], rt[h, g]), s[h]) for h in heads]
        uv = [cat(ws[h][:c] + u0[h, g], v[h, g]) for h in heads]
        y = [ws[h][c:] + _mm(m_bot[h, g], uv[h]) for h in heads]
        s = [s[h] * jnp.exp(cum_last[h, g]) + _mm_tn(uv[h], bk[h, g]) for h in heads]
        for h in heads:
            u = (h, g)
            sl = lanes(h)
            mu = jnp.mean(y[h], axis=-1, keepdims=True)
            yc = y[h] - mu
            var = jnp.mean(yc * yc, axis=-1, keepdims=True)
            yn = yc * lax.rsqrt(var + LN_X_EPS) * lnw_ref[:, sl] + lnb_ref[:, sl]
            bonus = jnp.sum(load(r_ref, u) * k[u] * rk_ref[:, sl], axis=-1, keepdims=True) * v[u]
            y_ref[c * g:c * (g + 1), sl] = (yn + bonus) * load(g_ref, u)
    for h in heads:
        s_ref[h] = s[h]


def _stage_rwkv_scan(seqs, consts, batch, seq):
    n = batch * seq
    d = RWKV_DIM
    step = CHUNK * SCAN_CHUNKS
    nc = seq // step
    blk = pl.BlockSpec((step, d), lambda b, c: (b * nc + c, 0))
    return pl.pallas_call(
        _rwkv_scan_kernel,
        grid=(batch, nc),
        in_specs=[blk] * 7 + [_full((1, d))] * 3,
        out_specs=blk,
        out_shape=jax.ShapeDtypeStruct((n, d), F32),
        scratch_shapes=[pltpu.VMEM((RWKV_HEADS, RWKV_HEAD, RWKV_HEAD), F32)],
        compiler_params=_cparams("parallel", "arbitrary"),
        name="rwkv_scan",
    )(*seqs, *consts)


def _rwkv_branch(pr, rwkv_mu, w0, w2, a0, a2, g2, k_k, k_a, r_k, ln_x_w, ln_x_b, batch, seq):
    d = RWKV_DIM
    row = lambda t: t.reshape(1, -1)
    w2p = jnp.concatenate([w2, jnp.zeros((A_LORA, d), F32)], axis=0)
    a2p = jnp.concatenate([jnp.zeros((W_LORA, d), F32), a2], axis=0)
    bones = jnp.kron(jnp.eye(RWKV_HEADS, dtype=F32), jnp.ones((RWKV_HEAD, RWKV_HEAD), F32))
    consts = (row(rwkv_mu), row(w0), row(a0), row(k_k), row(k_a), w2p, a2p, g2.astype(BF16), bones)
    seqs = _stage_rwkv_prep(pr, consts, seq, min(512, seq))
    return _stage_rwkv_scan(seqs, (row(r_k), row(ln_x_w), row(ln_x_b)), batch, seq)


def _outproj_kernel(x_ref, oa_ref, yb_ref, gao_ref, bones_ref, wo_ref, gffn_ref, wpq_ref, x1_ref, h2_ref, qp_ref):
    oa = oa_ref[...]
    ss = _dot((oa * oa).astype(BF16), bones_ref[...])
    ya = oa * lax.rsqrt(ss * (1.0 / MLA_V) + EPS) * gao_ref[...]
    mix = jnp.concatenate([ya, yb_ref[...]], axis=1).astype(BF16)
    x1 = x_ref[...] + _dot(mix, wo_ref[...])
    x1_ref[...] = x1
    ms = jnp.mean(x1 * x1, axis=-1, keepdims=True)
    h2 = x1 * lax.rsqrt(ms + EPS) * gffn_ref[...]
    h2_ref[...] = h2
    qp_ref[...] = _dot(h2.astype(BF16), wpq_ref[...]).astype(BF16)


def _stage_outproj(x2, o_attn, y_b, g_attn_out, w_o, g_ffn, w_pq, tm):
    n = x2.shape[0]
    qw = PEER_HEADS * PEER_DQ
    aw = MLA_HEADS * MLA_V
    bones = jnp.kron(jnp.eye(MLA_HEADS, dtype=F32), jnp.ones((MLA_V, MLA_V), F32)).astype(BF16)
    row = lambda w: pl.BlockSpec((tm, w), lambda i: (i, 0))
    return pl.pallas_call(
        _outproj_kernel,
        grid=(n // tm,),
        in_specs=[row(D_MODEL), row(aw), row(RWKV_DIM), _full((1, aw)), _full((aw, aw)),
                  _full((D_MODEL, D_MODEL)), _full((1, D_MODEL)), _full((D_MODEL, qw))],
        out_specs=[row(D_MODEL), row(D_MODEL), row(qw)],
        out_shape=[
            jax.ShapeDtypeStruct((n, D_MODEL), F32),
            jax.ShapeDtypeStruct((n, D_MODEL), F32),
            jax.ShapeDtypeStruct((n, qw), BF16),
        ],
        compiler_params=_cparams("parallel"),
        name="outproj",
    )(x2, o_attn, y_b, g_attn_out.reshape(1, aw), bones, w_o.astype(BF16), g_ffn.reshape(1, D_MODEL),
      w_pq.astype(BF16))


def _topk_rows(s, payload=None):
    rows = s.shape[0]
    riota = lax.broadcasted_iota(I32, s.shape, 0)
    vals, ids = [], []
    for _ in range(PEER_TOPK):
        m = jnp.max(s, axis=0, keepdims=True)
        pos = jnp.min(jnp.where(s == m, riota, rows), axis=0, keepdims=True)
        sel = riota == pos
        vals.append(m)
        ids.append(pos if payload is None else jnp.max(jnp.where(sel, payload, -1), axis=0, keepdims=True))
        s = jnp.where(sel, -jnp.inf, s)
    return jnp.concatenate(vals, axis=0), jnp.concatenate(ids, axis=0)


def _topk_keys(s):
    rows = s.shape[0]
    half = rows // 2
    riota = lax.broadcasted_iota(I32, (half, s.shape[1]), 0)
    a, b = s[:half], s[half:]
    first = a >= b
    win = jnp.where(first, a, b)
    lose = jnp.where(first, b, a)
    win_i = jnp.where(first, riota, riota + half)
    lose_i = jnp.where(first, riota + half, riota)
    vals, ids = [], []
    for _ in range(PEER_TOPK):
        m = jnp.max(win, axis=0, keepdims=True)
        pos = jnp.min(jnp.where(win == m, win_i, rows), axis=0, keepdims=True)
        sel = win_i == pos
        vals.append(m)
        ids.append(pos)
        win = jnp.where(sel, lose, win)
        win_i = jnp.where(sel, lose_i, win_i)
        lose = jnp.where(sel, -jnp.inf, lose)
    return jnp.concatenate(vals, axis=0), jnp.concatenate(ids, axis=0)


def _peer_route_kernel(qp_ref, keys_ref, idx_ref, gate_ref, ids_t, gate_t):
    h = pl.program_id(1)
    s1 = _dot_nt(keys_ref[0], qp_ref[:, :PEER_DHALF])
    s2 = _dot_nt(keys_ref[1], qp_ref[:, PEER_DHALF:])
    v1, i1 = _topk_keys(s1)
    v2, i2 = _topk_keys(s2)
    sub8 = lax.broadcasted_iota(I32, (8, s1.shape[1]), 0)
    i1 = i1 * (N_KEYS * EXPERT_ROWS)
    i2 = i2 * EXPERT_ROWS
    blocks_s = [v1[0:1] + v2]
    blocks_i = [i1[0:1] + i2]
    for i in range(1, 8):
        valid = sub8 < PEER_TOPK // (i + 1)
        blocks_s.append(jnp.where(valid, v1[i:i + 1] + v2[:8], -jnp.inf))
        blocks_i.append(i1[i:i + 1] + i2[:8])
    blocks_s.append(v1[8:] + v2[0:1])
    blocks_i.append(i1[8:] + i2[0:1])
    top_s, ids = _topk_rows(jnp.concatenate(blocks_s, axis=0), jnp.concatenate(blocks_i, axis=0))
    e = jnp.exp(top_s - top_s[0:1])
    gate = e / jnp.sum(e, axis=0, keepdims=True)
    r0 = pl.multiple_of(h * PEER_TOPK, PEER_TOPK)
    ids_t[pl.ds(r0, PEER_TOPK), :] = ids
    gate_t[pl.ds(r0, PEER_TOPK), :] = gate

    @pl.when(h == PEER_HEADS - 1)
    def _():
        idx_ref[...] = ids_t[...].T
        gate_ref[...] = gate_t[...].T


def _stage_peer_route(qp, sub_keys, tt):
    n = qp.shape[0]
    kw = PEER_HEADS * PEER_TOPK
    keys = sub_keys.reshape(PEER_HEADS * 2, N_KEYS, PEER_DHALF).astype(BF16)
    out = pl.BlockSpec((tt, kw), lambda i, h: (i, 0))
    return pl.pallas_call(
        _peer_route_kernel,
        grid=(n // tt, PEER_HEADS),
        in_specs=[
            pl.BlockSpec((tt, PEER_DQ), lambda i, h: (i, h)),
            pl.BlockSpec((2, N_KEYS, PEER_DHALF), lambda i, h: (h, 0, 0)),
        ],
        out_specs=[out, out],
        out_shape=[jax.ShapeDtypeStruct((n, kw), I32), jax.ShapeDtypeStruct((n, kw), F32)],
        scratch_shapes=[pltpu.VMEM((kw, tt), I32), pltpu.VMEM((kw, tt), F32)],
        compiler_params=_cparams("parallel", "arbitrary"),
        name="peer_route",
    )(qp, keys)


EXPERT_ROWS = 4
PAIRS = PEER_HEADS * PEER_TOPK


PACK_EXPERTS = 256


def _pack_kernel(t_ref, o_ref):
    rows = t_ref[...].reshape(PACK_EXPERTS * 8, 128)
    o_ref[...] = pltpu.bitcast(rows.astype(BF16), I32)


def _pack_table(t):
    n_exp = t.shape[0]
    return pl.pallas_call(
        _pack_kernel,
        grid=(n_exp // PACK_EXPERTS,),
        in_specs=[pl.BlockSpec((PACK_EXPERTS, D_MODEL), lambda i: (i, 0))],
        out_specs=pl.BlockSpec((PACK_EXPERTS * EXPERT_ROWS, 128), lambda i: (i, 0)),
        out_shape=jax.ShapeDtypeStruct((n_exp * EXPERT_ROWS, 128), I32),
        compiler_params=_cparams("parallel"),
        name="pack_table",
    )(t)


def _expert(tbl_ref, row):
    w = tbl_ref[pl.ds(pl.multiple_of(row, EXPERT_ROWS), EXPERT_ROWS), :]
    return pltpu.bitcast(w, BF16).astype(F32)


def _gelu(t):
    return 0.5 * t * (1.0 + lax.erf(t * (2.0 ** -0.5)))


TOKEN_UNROLL = 4
V_ACCS = 2
FOLD_ROWS = 4


def _peer_u_kernel(idx_ref, x_ref, gate_ref, tbl_ref, wgt_ref, prod_ref, psum_ref, act_ref):
    tb = x_ref.shape[0]

    def tokens(i, carry):
        for k in range(TOKEN_UNROLL):
            t = i * TOKEN_UNROLL + k
            xt = x_ref[pl.ds(t, 1), :].reshape(8, 128)
            base = t * PAIRS
            for p in range(PAIRS):
                prod = _expert(tbl_ref, idx_ref[base + p]) * xt
                prod_ref[k, FOLD_ROWS * p:FOLD_ROWS * (p + 1), :] = prod[:FOLD_ROWS] + prod[FOLD_ROWS:]
            part = prod_ref[k, pl.ds(0, PAIRS, stride=FOLD_ROWS), :]
            for s in range(1, FOLD_ROWS):
                part = part + prod_ref[k, pl.ds(s, PAIRS, stride=FOLD_ROWS), :]
            psum_ref[pl.ds(pl.multiple_of(t * PAIRS, PAIRS), PAIRS), :] = part
        return carry

    lax.fori_loop(0, tb // TOKEN_UNROLL, tokens, 0)
    ps = psum_ref[...]
    hi = ps.astype(BF16)
    lo = (ps - hi.astype(F32)).astype(BF16)
    ones = jnp.ones((8, 128), BF16)
    sums = _dot_nt(ones, hi) + _dot_nt(ones, lo)
    for t in range(tb):
        act_ref[t:t + 1, :] = sums[0:1, PAIRS * t:PAIRS * (t + 1)]
    wgt_ref[...] = gate_ref[...] * _gelu(act_ref[...])


REP_GROUP = 16


def _peer_v_kernel(idx_ref, wgt_ref, x1_ref, tbl_ref, out_ref, wrep_ref):
    tb = x1_ref.shape[0]
    eye = lax.broadcasted_iota(I32, (PAIRS, PAIRS), 0) == lax.broadcasted_iota(I32, (PAIRS, PAIRS), 1)
    ones = jnp.ones((PAIRS, 128), BF16)
    for g in range(tb // REP_GROUP):
        t0 = g * REP_GROUP
        d = jnp.concatenate([jnp.where(eye, wgt_ref[t:t + 1, :], 0.0) for t in range(t0, t0 + REP_GROUP)], axis=0)
        wrep_ref[t0 * PAIRS:(t0 + REP_GROUP) * PAIRS, :] = _dot(d.astype(BF16), ones)

    def token(t, carry):
        base = t * PAIRS
        accs = [jnp.zeros((8, 128), F32) for _ in range(V_ACCS)]
        for p in range(PAIRS):
            accs[p % V_ACCS] = (accs[p % V_ACCS]
                                + wrep_ref[pl.ds(base + p, 1), :] * _expert(tbl_ref, idx_ref[base + p]))
        out = x1_ref[pl.ds(t, 1), :].reshape(8, 128) + (accs[0] + accs[1])
        out_ref[pl.ds(t, 1), :] = out.reshape(1, D_MODEL)
        return carry

    lax.fori_loop(0, tb, token, 0)


def _table_spec(n_rows):
    return pl.BlockSpec((n_rows, 128), lambda i: (0, 0), pipeline_mode=pl.Buffered(1))


def _smem_spec(tb):
    return pl.BlockSpec((tb * PAIRS,), lambda i: (i,), memory_space=pltpu.SMEM)


def _stage_peer_u(idx, h2, gate, tbl, tb):
    n = h2.shape[0]
    tok = pl.BlockSpec((tb, D_MODEL), lambda i: (i, 0))
    row = pl.BlockSpec((tb, PAIRS), lambda i: (i, 0))
    return pl.pallas_call(
        _peer_u_kernel,
        grid=(n // tb,),
        in_specs=[_smem_spec(tb), tok, row, _table_spec(tbl.shape[0])],
        out_specs=row,
        out_shape=jax.ShapeDtypeStruct((n, PAIRS), F32),
        scratch_shapes=[pltpu.VMEM((TOKEN_UNROLL, PAIRS * FOLD_ROWS, 128), F32),
                        pltpu.VMEM((tb * PAIRS, 128), F32),
                        pltpu.VMEM((tb, PAIRS), F32)],
        compiler_params=_cparams("parallel"),
        name="peer_u",
    )(idx.reshape(-1), h2, gate, tbl)


def _stage_peer_v(idx, wgt, x1, tbl, tb):
    n = x1.shape[0]
    tok = pl.BlockSpec((tb, D_MODEL), lambda i: (i, 0))
    return pl.pallas_call(
        _peer_v_kernel,
        grid=(n // tb,),
        in_specs=[_smem_spec(tb), pl.BlockSpec((tb, PAIRS), lambda i: (i, 0)), tok, _table_spec(tbl.shape[0])],
        out_specs=tok,
        out_shape=jax.ShapeDtypeStruct((n, D_MODEL), F32),
        scratch_shapes=[pltpu.VMEM((tb * PAIRS, 128), F32)],
        compiler_params=_cparams("parallel"),
        name="peer_v",
    )(idx.reshape(-1), wgt, x1, tbl)


def _peer_block(x2, o_attn, y_b, g_attn_out, w_o, g_ffn, w_pq, sub_keys, expert_u, expert_v):
    n = x2.shape[0]
    x1, h2, qp = _stage_outproj(x2, o_attn, y_b, g_attn_out, w_o, g_ffn, w_pq, min(512, n))
    idx, gate = _stage_peer_route(qp, sub_keys, min(512, n))
    wgt = _stage_peer_u(idx, h2, gate, _pack_table(expert_u), min(128, n))
    return _stage_peer_v(idx, wgt, x1, _pack_table(expert_v), min(128, n))


def _mla_consts(g_cq, g_ckv, w_uq, w_uk, w_uv, g_qnorm, g_knorm):
    half = MLA_ROPE // 2
    inv = ROPE_THETA ** (-jnp.arange(half, dtype=F32) / half)
    z = lambda m: jnp.zeros((m,), F32)
    inv128 = jnp.concatenate([z(MLA_NOPE), inv, inv, z(HEAD_PAD - MLA_QK)]).reshape(1, HEAD_PAD)
    o = jnp.ones((half,), F32)
    m1 = jnp.concatenate([z(MLA_NOPE), -o, z(half), z(HEAD_PAD - MLA_QK)]).reshape(1, HEAD_PAD)
    m2 = jnp.concatenate([z(MLA_NOPE), z(half), o, z(HEAD_PAD - MLA_QK)]).reshape(1, HEAD_PAD)
    wuq = jnp.pad(w_uq.reshape(MLA_Q_RANK, MLA_HEADS, MLA_QK), ((0, 0), (0, 0), (0, HEAD_PAD - MLA_QK)))
    wuq = wuq.reshape(MLA_Q_RANK, MLA_HEADS * HEAD_PAD).astype(BF16)
    wuk = jnp.pad(w_uk.reshape(MLA_KV_RANK, MLA_HEADS, MLA_NOPE), ((0, 0), (0, 0), (0, HEAD_PAD - MLA_NOPE)))
    wuk = wuk.reshape(MLA_KV_RANK, MLA_HEADS * HEAD_PAD).astype(BF16)
    pad_g = lambda g: jnp.tile(jnp.pad(g, (0, HEAD_PAD - MLA_QK)), MLA_HEADS).reshape(1, MLA_HEADS * HEAD_PAD)
    bones = jnp.kron(jnp.eye(MLA_HEADS, dtype=F32), jnp.ones((HEAD_PAD, HEAD_PAD), F32)).astype(BF16)
    return (inv128, m1, m2, g_cq.reshape(1, -1), g_ckv.reshape(1, -1), wuq, wuk, w_uv.astype(BF16),
            pad_g(g_qnorm), pad_g(g_knorm), bones)


def _pad_w_in(w_in):
    z = jnp.zeros((D_MODEL, MLA_PAD - MLA_COLS), w_in.dtype)
    return jnp.concatenate([w_in[:, :MLA_COLS], z, w_in[:, MLA_COLS:]], axis=1).astype(BF16)


def _mla_branch(pm, positions, g_cq, g_ckv, w_uq, w_uk, w_uv, g_qnorm, g_knorm, batch, seq):
    n = batch * seq
    posf = positions.astype(F32).reshape(n, 1)
    consts = _mla_consts(g_cq, g_ckv, w_uq, w_uk, w_uv, g_qnorm, g_knorm)
    q, k, v = _stage_mla_prep(pm, posf, consts, min(512, n))
    return _stage_attn(q, k, v, batch, seq, min(1024, seq), min(1024, seq))


def kernel(x, positions, g_mix, w_in, rwkv_mu, g_cq, g_ckv, w_uq, w_uk, w_uv, g_qnorm, g_knorm, g_attn_out, w0, w2, a0, a2, g2, k_k, k_a, r_k, ln_x_w, ln_x_b, w_o, g_ffn, w_pq, sub_keys, expert_u, expert_v):
    batch, seq, _ = x.shape
    n = batch * seq
    x2 = x.reshape(n, D_MODEL)
    pm, pr = _stage_inproj(x2, g_mix[0], _pad_w_in(w_in[0]), min(512, n))
    o_attn = _mla_branch(pm, positions, g_cq[0], g_ckv[0], w_uq[0], w_uk[0], w_uv[0], g_qnorm[0], g_knorm[0],
                         batch, seq)
    y_b = _rwkv_branch(pr, rwkv_mu[0], w0[0], w2[0], a0[0], a2[0], g2[0], k_k[0], k_a[0], r_k[0], ln_x_w[0],
                       ln_x_b[0], batch, seq)
    out = _peer_block(x2, o_attn, y_b, g_attn_out[0], w_o[0], g_ffn[0], w_pq[0], sub_keys[0], expert_u[0],
                      expert_v[0])
    return out.reshape(batch, seq, D_MODEL)
```

```python
import functools

import jax
import jax.numpy as jnp
from jax import lax
from jax.experimental import pallas as pl
from jax.experimental.pallas import tpu as pltpu

F32 = jnp.float32
BF16 = jnp.bfloat16
I32 = jnp.int32

D_MODEL = 1024
CHUNK = 64
EPS = 1e-6

MLA_HEADS = 8
MLA_Q_RANK = 384
MLA_KV_RANK = 256
MLA_NOPE = 64
MLA_ROPE = 32
MLA_V = 64
MLA_QK = MLA_NOPE + MLA_ROPE
ROPE_THETA = 10000.0
MLA_COLS = MLA_Q_RANK + MLA_KV_RANK + MLA_ROPE
MLA_PAD = 768
HEAD_PAD = 128

RWKV_HEADS = 8
RWKV_HEAD = 64
RWKV_DIM = RWKV_HEADS * RWKV_HEAD
W_LORA = 64
A_LORA = 64
G_LORA = 128
LN_X_EPS = 64e-5
RWKV_COLS = 3 * RWKV_DIM + W_LORA + A_LORA + G_LORA

PEER_HEADS = 8
N_KEYS = 128
PEER_DQ = 256
PEER_DHALF = PEER_DQ // 2
PEER_TOPK = 16

V7X_VMEM_BYTES = 64 * 1024 * 1024
VMEM_LIMIT = V7X_VMEM_BYTES * 3 // 4


def _cparams(*sem):
    return pltpu.CompilerParams(dimension_semantics=sem, vmem_limit_bytes=VMEM_LIMIT)


def _full(shape):
    n = len(shape)
    return pl.BlockSpec(shape, lambda *_: (0,) * n)


def _dot(a, b):
    return jnp.dot(a, b, preferred_element_type=F32)


def _dot_nt(a, b):
    return lax.dot_general(a, b, (((1,), (1,)), ((), ())), preferred_element_type=F32)


def _inproj_kernel(x_ref, g_ref, w_ref, om_ref, or_ref):
    x = x_ref[...]
    ms = jnp.mean(x * x, axis=-1, keepdims=True)
    h = (x * lax.rsqrt(ms + EPS) * g_ref[...]).astype(BF16)
    om_ref[...] = _dot(h, w_ref[:, :MLA_PAD])
    or_ref[...] = _dot(h, w_ref[:, MLA_PAD:])


def _stage_inproj(x2, g_mix, w_in_p, tm):
    n = x2.shape[0]
    return pl.pallas_call(
        _inproj_kernel,
        grid=(n // tm,),
        in_specs=[
            pl.BlockSpec((tm, D_MODEL), lambda i: (i, 0)),
            _full((1, D_MODEL)),
            _full((D_MODEL, MLA_PAD + RWKV_COLS)),
        ],
        out_specs=[
            pl.BlockSpec((tm, MLA_PAD), lambda i: (i, 0)),
            pl.BlockSpec((tm, RWKV_COLS), lambda i: (i, 0)),
        ],
        out_shape=[
            jax.ShapeDtypeStruct((n, MLA_PAD), F32),
            jax.ShapeDtypeStruct((n, RWKV_COLS), F32),
        ],
        compiler_params=_cparams("parallel"),
        name="inproj",
    )(x2, g_mix.reshape(1, D_MODEL), w_in_p)


def _mla_prep_kernel(pm_ref, pos_ref, inv_ref, m1_ref, m2_ref, gcq_ref, gckv_ref, wuq_ref, wuk_ref, wuv_ref,
                     gq_ref, gk_ref, bones_ref, q_ref, k_ref, v_ref):
    def rms(t, g):
        ms = jnp.mean(t * t, axis=-1, keepdims=True)
        return (t * lax.rsqrt(ms + EPS) * g).astype(BF16)

    cqn = rms(pm_ref[:, :MLA_Q_RANK], gcq_ref[...])
    ckvn = rms(pm_ref[:, MLA_Q_RANK:MLA_Q_RANK + MLA_KV_RANK], gckv_ref[...])
    q = _dot(cqn, wuq_ref[...])
    k = _dot(ckvn, wuk_ref[...])
    v_ref[...] = _dot(ckvn, wuv_ref[...]).astype(BF16)
    kr = pltpu.roll(pm_ref[:, MLA_Q_RANK + MLA_KV_RANK:], MLA_NOPE, axis=1)
    k = k + jnp.concatenate([kr] * MLA_HEADS, axis=1)

    ang = pos_ref[...] * inv_ref[...]
    cos = jnp.cos(ang)
    sin = jnp.sin(ang)
    c_all = jnp.concatenate([cos] * MLA_HEADS, axis=1)
    s1_all = jnp.concatenate([sin * m1_ref[...]] * MLA_HEADS, axis=1)
    s2_all = jnp.concatenate([sin * m2_ref[...]] * MLA_HEADS, axis=1)
    half = MLA_ROPE // 2
    width = MLA_HEADS * HEAD_PAD

    def norm_rope(t, g):
        ss = _dot((t * t).astype(BF16), bones_ref[...])
        t = t * lax.rsqrt(ss * (1.0 / MLA_QK) + EPS) * g
        return t * c_all + pltpu.roll(t, width - half, axis=1) * s1_all + pltpu.roll(t, half, axis=1) * s2_all

    q_ref[...] = (norm_rope(q, gq_ref[...]) * (MLA_QK ** -0.5)).astype(BF16)
    k_ref[...] = norm_rope(k, gk_ref[...]).astype(BF16)


def _stage_mla_prep(pm, posf, consts, tm):
    n = pm.shape[0]
    width = MLA_HEADS * HEAD_PAD
    vw = MLA_HEADS * MLA_V
    row = lambda w: pl.BlockSpec((tm, w), lambda i: (i, 0))
    return pl.pallas_call(
        _mla_prep_kernel,
        grid=(n // tm,),
        in_specs=[row(MLA_PAD), row(1)] + [_full(c.shape) for c in consts],
        out_specs=[row(width), row(width), row(vw)],
        out_shape=[
            jax.ShapeDtypeStruct((n, width), BF16),
            jax.ShapeDtypeStruct((n, width), BF16),
            jax.ShapeDtypeStruct((n, vw), BF16),
        ],
        compiler_params=_cparams("parallel"),
        name="mla_prep",
    )(pm, posf, *consts)


def _attn_kernel(q_ref, k_ref, v_ref, o_ref, *, tq, tk):
    qi = pl.program_id(2)
    sub = tq // tk
    qs = [q_ref[:, HEAD_PAD * j:HEAD_PAD * (j + 1)] for j in range(2)]

    def tile(kt, carry, mask):
        ks = pl.multiple_of(kt * tk, tk)
        vb = v_ref[pl.ds(ks, tk), :]
        out = []
        for j in range(2):
            m, l, acc = carry[j]
            s = _dot_nt(qs[j], k_ref[pl.ds(ks, tk), HEAD_PAD * j:HEAD_PAD * (j + 1)])
            if mask is not None:
                s = jnp.where(mask, s, -jnp.inf)
            m_new = jnp.maximum(m, jnp.max(s, axis=-1, keepdims=True))
            alpha = jnp.exp(m - m_new)
            p = jnp.exp(s - m_new)
            l = alpha * l + jnp.sum(p, axis=-1, keepdims=True)
            acc = alpha * acc + _dot(p.astype(BF16), vb)
            out.append((m_new, l, acc))
        return tuple(out)

    one = (jnp.full((tq, 1), -jnp.inf, F32), jnp.zeros((tq, 1), F32), jnp.zeros((tq, 2 * MLA_V), F32))
    carry = lax.fori_loop(0, qi * sub, functools.partial(tile, mask=None), (one, one))
    row = lax.broadcasted_iota(I32, (tq, tk), 0)
    col = lax.broadcasted_iota(I32, (tq, tk), 1)
    for d in range(sub):
        carry = tile(qi * sub + d, carry, ((col + d * tk) // CHUNK) <= (row // CHUNK))
    lane = lax.broadcasted_iota(I32, (tq, 2 * MLA_V), 1)
    o_ref[...] = jnp.where(lane < MLA_V, carry[0][2] / carry[0][1], carry[1][2] / carry[1][1])


def _stage_attn(q, k, v, batch, seq, tq, tk):
    n = q.shape[0]
    nq = seq // tq
    return pl.pallas_call(
        functools.partial(_attn_kernel, tq=tq, tk=tk),
        grid=(batch, MLA_HEADS // 2, nq),
        in_specs=[
            pl.BlockSpec((tq, 2 * HEAD_PAD), lambda b, h, i: (b * nq + i, h)),
            pl.BlockSpec((seq, 2 * HEAD_PAD), lambda b, h, i: (b, h)),
            pl.BlockSpec((seq, 2 * MLA_V), lambda b, h, i: (b, h)),
        ],
        out_specs=pl.BlockSpec((tq, 2 * MLA_V), lambda b, h, i: (b * nq + i, h)),
        out_shape=jax.ShapeDtypeStruct((n, MLA_HEADS * MLA_V), F32),
        compiler_params=_cparams("parallel", "parallel", "arbitrary"),
        name="attn",
    )(q, k, v)


def _dot_hi(a, b):
    return jnp.dot(a, b, preferred_element_type=F32, precision=lax.Precision.HIGHEST)


def _sigmoid(t):
    return 1.0 / (1.0 + jnp.exp(-t))


def _rwkv_prep_kernel(pr_ref, mu_ref, w0_ref, a0_ref, kk_ref, ka_ref, w2_ref, a2_ref, g2_ref, bones_ref,
                      r_o, k_o, v_o, al_o, be_o, lw_o, g_o, carry_ref, *, tiles_per_seq):
    i = pl.program_id(0)
    p = pr_ref[...]
    tm = p.shape[0]

    @pl.when(i % tiles_per_seq == 0)
    def _():
        carry_ref[...] = jnp.zeros_like(carry_ref)

    rowi = lax.broadcasted_iota(I32, p.shape, 0)
    prev = jnp.where(rowi == 0, carry_ref[0:1, :], pltpu.roll(p, 1, axis=0))
    carry_ref[0:1, :] = p[tm - 1:tm, :]
    xs = p + (prev - p) * mu_ref[...]
    d = RWKV_DIM
    r = xs[:, :d]
    k = xs[:, d:2 * d]
    v = xs[:, 2 * d:3 * d]
    z = xs[:, 3 * d:3 * d + W_LORA + A_LORA]
    gl = xs[:, 3 * d + W_LORA + A_LORA:]
    lane = lax.broadcasted_iota(I32, z.shape, 1)
    zt = jnp.where(lane < W_LORA, jnp.tanh(z), z)
    t = -(w0_ref[...] + _dot_hi(zt, w2_ref[...]))
    w = -(jnp.maximum(t, 0.0) + jnp.log1p(jnp.exp(-jnp.abs(t)))) - 0.5
    a = _sigmoid(a0_ref[...] + _dot_hi(zt, a2_ref[...]))
    kk = k * kk_ref[...]
    sq = kk * kk
    sq_hi = sq.astype(BF16)
    sq_lo = (sq - sq_hi.astype(F32)).astype(BF16)
    blocks = bones_ref[...].astype(BF16)
    ss = _dot(sq_hi, blocks) + _dot(sq_lo, blocks)
    kk = kk / jnp.maximum(jnp.sqrt(ss), 1e-12)
    r_o[...] = r
    k_o[...] = k * (1.0 + (a - 1.0) * ka_ref[...])
    v_o[...] = v
    al_o[...] = -kk
    be_o[...] = kk * a
    lw_o[...] = -jnp.exp(w)
    g_o[...] = _dot(_sigmoid(gl).astype(BF16), g2_ref[...])


def _stage_rwkv_prep(pr, consts, seq, tm):
    n = pr.shape[0]
    d = RWKV_DIM
    row = lambda w: pl.BlockSpec((tm, w), lambda i: (i, 0))
    return pl.pallas_call(
        functools.partial(_rwkv_prep_kernel, tiles_per_seq=seq // tm),
        grid=(n // tm,),
        in_specs=[row(RWKV_COLS)] + [_full(c.shape) for c in consts],
        out_specs=[row(d)] * 7,
        out_shape=[jax.ShapeDtypeStruct((n, d), F32)] * 7,
        scratch_shapes=[pltpu.VMEM((8, RWKV_COLS), F32)],
        compiler_params=_cparams("arbitrary"),
        name="rwkv_prep",
    )(pr, *consts)


def _mm(a, b):
    return jnp.dot(a.astype(BF16), b.astype(BF16), preferred_element_type=F32)


def _mm_nt(a, b):
    return _dot_nt(a.astype(BF16), b.astype(BF16))


def _mm_tn(a, b):
    return lax.dot_general(a.astype(BF16), b.astype(BF16), (((0,), (0,)), ((), ())), preferred_element_type=F32)


SCAN_CHUNKS = 4


def _rwkv_scan_kernel(r_ref, k_ref, v_ref, al_ref, be_ref, lw_ref, g_ref, rk_ref, lnw_ref, lnb_ref, y_ref, s_ref):
    c = CHUNK
    hd = RWKV_HEAD

    @pl.when(pl.program_id(1) == 0)
    def _():
        s_ref[...] = jnp.zeros_like(s_ref)

    row = lax.broadcasted_iota(I32, (c, c), 0)
    col = lax.broadcasted_iota(I32, (c, c), 1)
    tri = jnp.where(col <= row, 1.0, 0.0).astype(F32)
    eye = jnp.where(col == row, 1.0, 0.0).astype(F32)
    row2 = lax.broadcasted_iota(I32, (2 * c, 2 * c), 0)
    col2 = lax.broadcasted_iota(I32, (2 * c, 2 * c), 1) % c
    keep = col2 < jnp.where(row2 < c, row2, row2 - c + 1)
    heads = range(RWKV_HEADS)
    units = [(h, g) for g in range(SCAN_CHUNKS) for h in heads]
    cat = lambda a, b: jnp.concatenate([a, b], axis=0)

    def lanes(h):
        return slice(hd * h, hd * (h + 1))

    def load(ref, u):
        return ref[c * u[1]:c * (u[1] + 1), lanes(u[0])]

    lw = {u: load(lw_ref, u) for u in units}
    cum = {u: _dot_hi(tri, lw[u]) for u in units}
    k = {u: load(k_ref, u) for u in units}
    v = {u: load(v_ref, u) for u in units}
    be = {u: load(be_ref, u) for u in units}
    rt = {u: load(r_ref, u) * jnp.exp(cum[u]) for u in units}
    at = {u: load(al_ref, u) * jnp.exp(cum[u] - lw[u]) for u in units}
    e_neg = {u: jnp.exp(-cum[u]) for u in units}
    m = {u: jnp.where(keep, _mm_nt(cat(at[u], rt[u]), cat(be[u] * e_neg[u], k[u] * e_neg[u])), 0.0) for u in units}
    m_top = {u: m[u][:c] for u in units}
    m_bot = {u: m[u][c:] for u in units}
    pw = {u: m_top[u][:, :c] for u in units}
    x = {u: eye + pw[u] for u in units}
    for _ in range(max(1, (c - 1).bit_length() - 1)):
        pw = {u: _mm(pw[u], pw[u]) for u in units}
        x = {u: x[u] + _mm(x[u], pw[u]) for u in units}
    w_mat = {u: _mm(x[u], at[u]) for u in units}
    akv = {u: _mm(m_top[u], cat(jnp.zeros_like(v[u]), v[u])) for u in units}
    u0 = {u: _mm(x[u], akv[u]) for u in units}
    cum_last = {u: cum[u][c - 1:c, :] for u in units}
    bk = {}
    for u in units:
        e_last = jnp.exp(cum_last[u] - cum[u])
        bk[u] = cat(be[u] * e_last, k[u] * e_last)

    s = [s_ref[h] for h in heads]
    for g in range(SCAN_CHUNKS):
        ws = [_mm_nt(cat(w_mat[h, g], rt[h, g]), s[h]) for h in heads]
        uv = [cat(ws[h][:c] + u0[h, g], v[h, g]) for h in heads]
        y = [ws[h][c:] + _mm(m_bot[h, g], uv[h]) for h in heads]
        s = [s[h] * jnp.exp(cum_last[h, g]) + _mm_tn(uv[h], bk[h, g]) for h in heads]
        for h in heads:
            u = (h, g)
            sl = lanes(h)
            mu = jnp.mean(y[h], axis=-1, keepdims=True)
            yc = y[h] - mu
            var = jnp.mean(yc * yc, axis=-1, keepdims=True)
            yn = yc * lax.rsqrt(var + LN_X_EPS) * lnw_ref[:, sl] + lnb_ref[:, sl]
            bonus = jnp.sum(load(r_ref, u) * k[u] * rk_ref[:, sl], axis=-1, keepdims=True) * v[u]
            y_ref[c * g:c * (g + 1), sl] = (yn + bonus) * load(g_ref, u)
    for h in heads:
        s_ref[h] = s[h]


def _stage_rwkv_scan(seqs, consts, batch, seq):
    n = batch * seq
    d = RWKV_DIM
    step = CHUNK * SCAN_CHUNKS
    nc = seq // step
    blk = pl.BlockSpec((step, d), lambda b, c: (b * nc + c, 0))
    return pl.pallas_call(
        _rwkv_scan_kernel,
        grid=(batch, nc),
        in_specs=[blk] * 7 + [_full((1, d))] * 3,
        out_specs=blk,
        out_shape=jax.ShapeDtypeStruct((n, d), F32),
        scratch_shapes=[pltpu.VMEM((RWKV_HEADS, RWKV_HEAD, RWKV_HEAD), F32)],
        compiler_params=_cparams("parallel", "arbitrary"),
        name="rwkv_scan",
    )(*seqs, *consts)


def _rwkv_branch(pr, rwkv_mu, w0, w2, a0, a2, g2, k_k, k_a, r_k, ln_x_w, ln_x_b, batch, seq):
    d = RWKV_DIM
    row = lambda t: t.reshape(1, -1)
    w2p = jnp.concatenate([w2, jnp.zeros((A_LORA, d), F32)], axis=0)
    a2p = jnp.concatenate([jnp.zeros((W_LORA, d), F32), a2], axis=0)
    bones = jnp.kron(jnp.eye(RWKV_HEADS, dtype=F32), jnp.ones((RWKV_HEAD, RWKV_HEAD), F32))
    consts = (row(rwkv_mu), row(w0), row(a0), row(k_k), row(k_a), w2p, a2p, g2.astype(BF16), bones)
    seqs = _stage_rwkv_prep(pr, consts, seq, min(512, seq))
    return _stage_rwkv_scan(seqs, (row(r_k), row(ln_x_w), row(ln_x_b)), batch, seq)


def _outproj_kernel(x_ref, oa_ref, yb_ref, gao_ref, bones_ref, wo_ref, gffn_ref, wpq_ref, x1_ref, h2_ref, qp_ref):
    oa = oa_ref[...]
    ss = _dot((oa * oa).astype(BF16), bones_ref[...])
    ya = oa * lax.rsqrt(ss * (1.0 / MLA_V) + EPS) * gao_ref[...]
    mix = jnp.concatenate([ya, yb_ref[...]], axis=1).astype(BF16)
    x1 = x_ref[...] + _dot(mix, wo_ref[...])
    x1_ref[...] = x1
    ms = jnp.mean(x1 * x1, axis=-1, keepdims=True)
    h2 = x1 * lax.rsqrt(ms + EPS) * gffn_ref[...]
    h2_ref[...] = h2
    qp_ref[...] = _dot(h2.astype(BF16), wpq_ref[...]).astype(BF16)


def _stage_outproj(x2, o_attn, y_b, g_attn_out, w_o, g_ffn, w_pq, tm):
    n = x2.shape[0]
    qw = PEER_HEADS * PEER_DQ
    aw = MLA_HEADS * MLA_V
    bones = jnp.kron(jnp.eye(MLA_HEADS, dtype=F32), jnp.ones((MLA_V, MLA_V), F32)).astype(BF16)
    row = lambda w: pl.BlockSpec((tm, w), lambda i: (i, 0))
    return pl.pallas_call(
        _outproj_kernel,
        grid=(n // tm,),
        in_specs=[row(D_MODEL), row(aw), row(RWKV_DIM), _full((1, aw)), _full((aw, aw)),
                  _full((D_MODEL, D_MODEL)), _full((1, D_MODEL)), _full((D_MODEL, qw))],
        out_specs=[row(D_MODEL), row(D_MODEL), row(qw)],
        out_shape=[
            jax.ShapeDtypeStruct((n, D_MODEL), F32),
            jax.ShapeDtypeStruct((n, D_MODEL), F32),
            jax.ShapeDtypeStruct((n, qw), BF16),
        ],
        compiler_params=_cparams("parallel"),
        name="outproj",
    )(x2, o_attn, y_b, g_attn_out.reshape(1, aw), bones, w_o.astype(BF16), g_ffn.reshape(1, D_MODEL),
      w_pq.astype(BF16))


def _topk_rows(s, payload=None):
    rows = s.shape[0]
    riota = lax.broadcasted_iota(I32, s.shape, 0)
    vals, ids = [], []
    for _ in range(PEER_TOPK):
        m = jnp.max(s, axis=0, keepdims=True)
        pos = jnp.min(jnp.where(s == m, riota, rows), axis=0, keepdims=True)
        sel = riota == pos
        vals.append(m)
        ids.append(pos if payload is None else jnp.max(jnp.where(sel, payload, -1), axis=0, keepdims=True))
        s = jnp.where(sel, -jnp.inf, s)
    return jnp.concatenate(vals, axis=0), jnp.concatenate(ids, axis=0)


def _topk_keys(s):
    rows = s.shape[0]
    half = rows // 2
    riota = lax.broadcasted_iota(I32, (half, s.shape[1]), 0)
    a, b = s[:half], s[half:]
    first = a >= b
    win = jnp.where(first, a, b)
    lose = jnp.where(first, b, a)
    win_i = jnp.where(first, riota, riota + half)
    lose_i = jnp.where(first, riota + half, riota)
    vals, ids = [], []
    for _ in range(PEER_TOPK):
        m = jnp.max(win, axis=0, keepdims=True)
        pos = jnp.min(jnp.where(win == m, win_i, rows), axis=0, keepdims=True)
        sel = win_i == pos
        vals.append(m)
        ids.append(pos)
        win = jnp.where(sel, lose, win)
        win_i = jnp.where(sel, lose_i, win_i)
        lose = jnp.where(sel, -jnp.inf, lose)
    return jnp.concatenate(vals, axis=0), jnp.concatenate(ids, axis=0)


def _peer_route_kernel(qp_ref, keys_ref, idx_ref, gate_ref, ids_t, gate_t):
    h = pl.program_id(1)
    s1 = _dot_nt(keys_ref[0], qp_ref[:, :PEER_DHALF])
    s2 = _dot_nt(keys_ref[1], qp_ref[:, PEER_DHALF:])
    v1, i1 = _topk_keys(s1)
    v2, i2 = _topk_keys(s2)
    sub8 = lax.broadcasted_iota(I32, (8, s1.shape[1]), 0)
    i1 = i1 * (N_KEYS * EXPERT_ROWS)
    i2 = i2 * EXPERT_ROWS
    blocks_s = [v1[0:1] + v2]
    blocks_i = [i1[0:1] + i2]
    for i in range(1, 8):
        valid = sub8 < PEER_TOPK // (i + 1)
        blocks_s.append(jnp.where(valid, v1[i:i + 1] + v2[:8], -jnp.inf))
        blocks_i.append(i1[i:i + 1] + i2[:8])
    blocks_s.append(v1[8:] + v2[0:1])
    blocks_i.append(i1[8:] + i2[0:1])
    top_s, ids = _topk_rows(jnp.concatenate(blocks_s, axis=0), jnp.concatenate(blocks_i, axis=0))
    e = jnp.exp(top_s - top_s[0:1])
    gate = e / jnp.sum(e, axis=0, keepdims=True)
    r0 = pl.multiple_of(h * PEER_TOPK, PEER_TOPK)
    ids_t[pl.ds(r0, PEER_TOPK), :] = ids
    gate_t[pl.ds(r0, PEER_TOPK), :] = gate

    @pl.when(h == PEER_HEADS - 1)
    def _():
        idx_ref[...] = ids_t[...].T
        gate_ref[...] = gate_t[...].T


def _stage_peer_route(qp, sub_keys, tt):
    n = qp.shape[0]
    kw = PEER_HEADS * PEER_TOPK
    keys = sub_keys.reshape(PEER_HEADS * 2, N_KEYS, PEER_DHALF).astype(BF16)
    out = pl.BlockSpec((tt, kw), lambda i, h: (i, 0))
    return pl.pallas_call(
        _peer_route_kernel,
        grid=(n // tt, PEER_HEADS),
        in_specs=[
            pl.BlockSpec((tt, PEER_DQ), lambda i, h: (i, h)),
            pl.BlockSpec((2, N_KEYS, PEER_DHALF), lambda i, h: (h, 0, 0)),
        ],
        out_specs=[out, out],
        out_shape=[jax.ShapeDtypeStruct((n, kw), I32), jax.ShapeDtypeStruct((n, kw), F32)],
        scratch_shapes=[pltpu.VMEM((kw, tt), I32), pltpu.VMEM((kw, tt), F32)],
        compiler_params=_cparams("parallel", "arbitrary"),
        name="peer_route",
    )(qp, keys)


EXPERT_ROWS = 4
PAIRS = PEER_HEADS * PEER_TOPK


PACK_EXPERTS = 256


def _pack_kernel(t_ref, o_ref):
    rows = t_ref[...].reshape(PACK_EXPERTS * 8, 128)
    o_ref[...] = pltpu.bitcast(rows.astype(BF16), I32)


def _pack_table(t):
    n_exp = t.shape[0]
    return pl.pallas_call(
        _pack_kernel,
        grid=(n_exp // PACK_EXPERTS,),
        in_specs=[pl.BlockSpec((PACK_EXPERTS, D_MODEL), lambda i: (i, 0))],
        out_specs=pl.BlockSpec((PACK_EXPERTS * EXPERT_ROWS, 128), lambda i: (i, 0)),
        out_shape=jax.ShapeDtypeStruct((n_exp * EXPERT_ROWS, 128), I32),
        compiler_params=_cparams("parallel"),
        name="pack_table",
    )(t)


def _expert(tbl_ref, row):
    w = tbl_ref[pl.ds(pl.multiple_of(row, EXPERT_ROWS), EXPERT_ROWS), :]
    return pltpu.bitcast(w, BF16).astype(F32)


def _gelu(t):
    return 0.5 * t * (1.0 + lax.erf(t * (2.0 ** -0.5)))


TOKEN_UNROLL = 4
V_ACCS = 2
FOLD_ROWS = 4


def _peer_u_kernel(idx_ref, x_ref, gate_ref, tbl_ref, wgt_ref, prod_ref, psum_ref, act_ref):
    tb = x_ref.shape[0]

    def tokens(i, carry):
        for k in range(TOKEN_UNROLL):
            t = i * TOKEN_UNROLL + k
            xt = x_ref[pl.ds(t, 1), :].reshape(8, 128)
            base = t * PAIRS
            for p in range(PAIRS):
                prod = _expert(tbl_ref, idx_ref[base + p]) * xt
                prod_ref[k, FOLD_ROWS * p:FOLD_ROWS * (p + 1), :] = prod[:FOLD_ROWS] + prod[FOLD_ROWS:]
            part = prod_ref[k, pl.ds(0, PAIRS, stride=FOLD_ROWS), :]
            for s in range(1, FOLD_ROWS):
                part = part + prod_ref[k, pl.ds(s, PAIRS, stride=FOLD_ROWS), :]
            psum_ref[pl.ds(pl.multiple_of(t * PAIRS, PAIRS), PAIRS), :] = part
        return carry

    lax.fori_loop(0, tb // TOKEN_UNROLL, tokens, 0)
    ps = psum_ref[...]
    hi = ps.astype(BF16)
    lo = (ps - hi.astype(F32)).astype(BF16)
    ones = jnp.ones((8, 128), BF16)
    sums = _dot_nt(ones, hi) + _dot_nt(ones, lo)
    for t in range(tb):
        act_ref[t:t + 1, :] = sums[0:1, PAIRS * t:PAIRS * (t + 1)]
    wgt_ref[...] = gate_ref[...] * _gelu(act_ref[...])


REP_GROUP = 16


def _peer_v_kernel(idx_ref, wgt_ref, x1_ref, tbl_ref, out_ref, wrep_ref):
    tb = x1_ref.shape[0]
    eye = lax.broadcasted_iota(I32, (PAIRS, PAIRS), 0) == lax.broadcasted_iota(I32, (PAIRS, PAIRS), 1)
    ones = jnp.ones((PAIRS, 128), BF16)
    for g in range(tb // REP_GROUP):
        t0 = g * REP_GROUP
        d = jnp.concatenate([jnp.where(eye, wgt_ref[t:t + 1, :], 0.0) for t in range(t0, t0 + REP_GROUP)], axis=0)
        wrep_ref[t0 * PAIRS:(t0 + REP_GROUP) * PAIRS, :] = _dot(d.astype(BF16), ones)

    def token(t, carry):
        base = t * PAIRS
        accs = [jnp.zeros((8, 128), F32) for _ in range(V_ACCS)]
        for p in range(PAIRS):
            accs[p % V_ACCS] = (accs[p % V_ACCS]
                                + wrep_ref[pl.ds(base + p, 1), :] * _expert(tbl_ref, idx_ref[base + p]))
        out = x1_ref[pl.ds(t, 1), :].reshape(8, 128) + (accs[0] + accs[1])
        out_ref[pl.ds(t, 1), :] = out.reshape(1, D_MODEL)
        return carry

    lax.fori_loop(0, tb, token, 0)


def _table_spec(n_rows):
    return pl.BlockSpec((n_rows, 128), lambda i: (0, 0), pipeline_mode=pl.Buffered(1))


def _smem_spec(tb):
    return pl.BlockSpec((tb * PAIRS,), lambda i: (i,), memory_space=pltpu.SMEM)


def _stage_peer_u(idx, h2, gate, tbl, tb):
    n = h2.shape[0]
    tok = pl.BlockSpec((tb, D_MODEL), lambda i: (i, 0))
    row = pl.BlockSpec((tb, PAIRS), lambda i: (i, 0))
    return pl.pallas_call(
        _peer_u_kernel,
        grid=(n // tb,),
        in_specs=[_smem_spec(tb), tok, row, _table_spec(tbl.shape[0])],
        out_specs=row,
        out_shape=jax.ShapeDtypeStruct((n, PAIRS), F32),
        scratch_shapes=[pltpu.VMEM((TOKEN_UNROLL, PAIRS * FOLD_ROWS, 128), F32),
                        pltpu.VMEM((tb * PAIRS, 128), F32),
                        pltpu.VMEM((tb, PAIRS), F32)],
        compiler_params=_cparams("parallel"),
        name="peer_u",
    )(idx.reshape(-1), h2, gate, tbl)


def _stage_peer_v(idx, wgt, x1, tbl, tb):
    n = x1.shape[0]
    tok = pl.BlockSpec((tb, D_MODEL), lambda i: (i, 0))
    return pl.pallas_call(
        _peer_v_kernel,
        grid=(n // tb,),
        in_specs=[_smem_spec(tb), pl.BlockSpec((tb, PAIRS), lambda i: (i, 0)), tok, _table_spec(tbl.shape[0])],
        out_specs=tok,
        out_shape=jax.ShapeDtypeStruct((n, D_MODEL), F32),
        scratch_shapes=[pltpu.VMEM((tb * PAIRS, 128), F32)],
        compiler_params=_cparams("parallel"),
        name="peer_v",
    )(idx.reshape(-1), wgt, x1, tbl)


def _peer_block(x2, o_attn, y_b, g_attn_out, w_o, g_ffn, w_pq, sub_keys, expert_u, expert_v):
    n = x2.shape[0]
    x1, h2, qp = _stage_outproj(x2, o_attn, y_b, g_attn_out, w_o, g_ffn, w_pq, min(512, n))
    idx, gate = _stage_peer_route(qp, sub_keys, min(512, n))
    wgt = _stage_peer_u(idx, h2, gate, _pack_table(expert_u), min(128, n))
    return _stage_peer_v(idx, wgt, x1, _pack_table(expert_v), min(128, n))


def _mla_consts(g_cq, g_ckv, w_uq, w_uk, w_uv, g_qnorm, g_knorm):
    half = MLA_ROPE // 2
    inv = ROPE_THETA ** (-jnp.arange(half, dtype=F32) / half)
    z = lambda m: jnp.zeros((m,), F32)
    inv128 = jnp.concatenate([z(MLA_NOPE), inv, inv, z(HEAD_PAD - MLA_QK)]).reshape(1, HEAD_PAD)
    o = jnp.ones((half,), F32)
    m1 = jnp.concatenate([z(MLA_NOPE), -o, z(half), z(HEAD_PAD - MLA_QK)]).reshape(1, HEAD_PAD)
    m2 = jnp.concatenate([z(MLA_NOPE), z(half), o, z(HEAD_PAD - MLA_QK)]).reshape(1, HEAD_PAD)
    wuq = jnp.pad(w_uq.reshape(MLA_Q_RANK, MLA_HEADS, MLA_QK), ((0, 0), (0, 0), (0, HEAD_PAD - MLA_QK)))
    wuq = wuq.reshape(MLA_Q_RANK, MLA_HEADS * HEAD_PAD).astype(BF16)
    wuk = jnp.pad(w_uk.reshape(MLA_KV_RANK, MLA_HEADS, MLA_NOPE), ((0, 0), (0, 0), (0, HEAD_PAD - MLA_NOPE)))
    wuk = wuk.reshape(MLA_KV_RANK, MLA_HEADS * HEAD_PAD).astype(BF16)
    pad_g = lambda g: jnp.tile(jnp.pad(g, (0, HEAD_PAD - MLA_QK)), MLA_HEADS).reshape(1, MLA_HEADS * HEAD_PAD)
    bones = jnp.kron(jnp.eye(MLA_HEADS, dtype=F32), jnp.ones((HEAD_PAD, HEAD_PAD), F32)).astype(BF16)
    return (inv128, m1, m2, g_cq.reshape(1, -1), g_ckv.reshape(1, -1), wuq, wuk, w_uv.astype(BF16),
            pad_g(g_qnorm), pad_g(g_knorm), bones)


def _pad_w_in(w_in):
    z = jnp.zeros((D_MODEL, MLA_PAD - MLA_COLS), w_in.dtype)
    return jnp.concatenate([w_in[:, :MLA_COLS], z, w_in[:, MLA_COLS:]], axis=1).astype(BF16)


def _mla_branch(pm, positions, g_cq, g_ckv, w_uq, w_uk, w_uv, g_qnorm, g_knorm, batch, seq):
    n = batch * seq
    posf = positions.astype(F32).reshape(n, 1)
    consts = _mla_consts(g_cq, g_ckv, w_uq, w_uk, w_uv, g_qnorm, g_knorm)
    q, k, v = _stage_mla_prep(pm, posf, consts, min(512, n))
    return _stage_attn(q, k, v, batch, seq, min(1024, seq), min(1024, seq))


def kernel(x, positions, g_mix, w_in, rwkv_mu, g_cq, g_ckv, w_uq, w_uk, w_uv, g_qnorm, g_knorm, g_attn_out, w0, w2, a0, a2, g2, k_k, k_a, r_k, ln_x_w, ln_x_b, w_o, g_ffn, w_pq, sub_keys, expert_u, expert_v):
    batch, seq, _ = x.shape
    n = batch * seq
    x2 = x.reshape(n, D_MODEL)
    pm, pr = _stage_inproj(x2, g_mix[0], _pad_w_in(w_in[0]), min(512, n))
    o_attn = _mla_branch(pm, positions, g_cq[0], g_ckv[0], w_uq[0], w_uk[0], w_uv[0], g_qnorm[0], g_knorm[0],
                         batch, seq)
    y_b = _rwkv_branch(pr, rwkv_mu[0], w0[0], w2[0], a0[0], a2[0], g2[0], k_k[0], k_a[0], r_k[0], ln_x_w[0],
                       ln_x_b[0], batch, seq)
    out = _peer_block(x2, o_attn, y_b, g_attn_out[0], w_o[0], g_ffn[0], w_pq[0], sub_keys[0], expert_u[0],
                      expert_v[0])
    return out.reshape(batch, seq, D_MODEL)
```
